```python
import jax, jax.numpy as jnp
from jax import lax
import numpy as np

D_MODEL = 2048
BATCH = 4
SEQ = 4096
DEPTH = 2

GRID_W = 64
CTX_LEN = 256
D_MIX = D_MODEL
SGU_WIDTH = D_MIX // 2
SGU_GROUPS = 8
SGU_GROUP_DIM = SGU_WIDTH // SGU_GROUPS
CHUNK = 128
ATTN_WIDTH = D_MIX - SGU_WIDTH
HEAD_DIM = 128
N_Q_HEADS = ATTN_WIDTH // HEAD_DIM
N_KV_HEADS = 2
KV_WIDTH = N_KV_HEADS * HEAD_DIM
Q_BLOCK = 128
ROPE_THETA = 10000.0
EPS = 1e-6
SPLIT_POINTS = (
    SGU_WIDTH,
    2 * SGU_WIDTH,
    3 * SGU_WIDTH,
    3 * SGU_WIDTH + ATTN_WIDTH,
    3 * SGU_WIDTH + ATTN_WIDTH + KV_WIDTH,
    3 * SGU_WIDTH + ATTN_WIDTH + 2 * KV_WIDTH,
)
KV_START = SPLIT_POINTS[3]
KV_END = SPLIT_POINTS[5]
D_IN = SPLIT_POINTS[5] + ATTN_WIDTH

kernel_name = "hybrid_sgu_gqa_prefix_dit_block"


def rms_norm(x, w):
    xf = x.astype(jnp.float32)
    y = xf * lax.rsqrt(jnp.mean(xf * xf, axis=-1, keepdims=True) + EPS)
    return (y * w.astype(jnp.float32)).astype(x.dtype)


def modulation(cond, w_mod, b_mod):
    m = jax.nn.silu(cond) @ w_mod + b_mod
    m = m.reshape(-1, 1, 3 * D_MODEL)
    return jnp.split(m, 3, axis=-1)


def axial_rope_tables(n_tokens, dtype):
    rows = n_tokens // GRID_W
    row_id = jnp.broadcast_to(jnp.arange(rows)[:, None], (rows, GRID_W)).reshape(-1)
    col_id = jnp.broadcast_to(jnp.arange(GRID_W)[None, :], (rows, GRID_W)).reshape(-1)
    axis_dim = HEAD_DIM // 2
    inv_freq = ROPE_THETA ** (-jnp.arange(0, axis_dim, 2, dtype=jnp.float32) / axis_dim)
    ang_r = row_id.astype(jnp.float32)[:, None] * inv_freq[None, :]
    ang_c = col_id.astype(jnp.float32)[:, None] * inv_freq[None, :]
    ang = jnp.concatenate([ang_r, ang_r, ang_c, ang_c], axis=-1)
    return jnp.cos(ang).astype(dtype), jnp.sin(ang).astype(dtype)


def _rotate_half(t):
    t1, t2 = jnp.split(t, 2, axis=-1)
    return jnp.concatenate([-t2, t1], axis=-1)


def apply_axial_rope(x, cos, sin):
    x_r, x_c = jnp.split(x, 2, axis=-1)
    rot = jnp.concatenate([_rotate_half(x_r), _rotate_half(x_c)], axis=-1)
    return x * cos + rot * sin


def split_heads(t, n_heads):
    b, n, _ = t.shape
    return t.reshape(b, n, n_heads, HEAD_DIM).transpose(0, 2, 1, 3)


def merge_heads(t):
    b, h, n, d = t.shape
    return t.transpose(0, 2, 1, 3).reshape(b, n, h * d)


def kv_heads(k, v, k_norm_w):
    return rms_norm(split_heads(k, N_KV_HEADS), k_norm_w), split_heads(v, N_KV_HEADS)


def blocked_gqa(q, k, v):
    b, hq, nq, d = q.shape
    rep = hq // N_KV_HEADS
    qb = q.reshape(b, N_KV_HEADS, rep, nq // Q_BLOCK, Q_BLOCK, d)
    qb = jnp.moveaxis(qb, 3, 0)
    scale = HEAD_DIM ** -0.5

    def one_block(q_blk):
        s = jnp.einsum('bgrqd,bgkd->bgrqk', q_blk, k, preferred_element_type=jnp.float32) * scale
        p = jax.nn.softmax(s, axis=-1).astype(v.dtype)
        return jnp.einsum('bgrqk,bgkd->bgrqd', p, v)

    o = lax.map(one_block, qb)
    return jnp.moveaxis(o, 0, 3).reshape(b, hq, nq, d)


def chunk_sgu(u, v, w_sgu, b_sgu, v_norm_w):
    b, n, _ = u.shape
    u = jax.nn.gelu(u)
    v = jax.nn.gelu(v).reshape(b, n // CHUNK, CHUNK, SGU_GROUPS, SGU_GROUP_DIM)
    v = rms_norm(v, v_norm_w)
    s = jnp.einsum('gpq,bcqgd->bcpgd', w_sgu, v) + b_sgu.T[:, :, None]
    return u * s.reshape(b, n, SGU_WIDTH)


def setup_inputs(seed: int = 0) -> dict:
    key = jax.random.key(seed)
    ks = jax.random.split(key, 16)
    f32 = jnp.float32
    nrm = lambda k, shape: jax.random.normal(k, shape, f32)
    return {
        "x": nrm(ks[0], (BATCH, SEQ, D_MODEL)),
        "c": nrm(ks[1], (BATCH, D_MODEL)),
        "ctx": nrm(ks[2], (BATCH, CTX_LEN, D_MODEL)),
        "c_ctx": nrm(ks[3], (D_MODEL,)),
        "norm_w": 1.0 + 0.02 * nrm(ks[4], (DEPTH, D_MODEL)),
        "w_mod": 0.5 * D_MODEL ** -0.5 * nrm(ks[5], (DEPTH, D_MODEL, 3 * D_MODEL)),
        "b_mod": 0.01 * nrm(ks[6], (DEPTH, 3 * D_MODEL)),
        "w_in": D_MODEL ** -0.5 * nrm(ks[7], (DEPTH, D_MODEL, D_IN)),
        "w_sgu": CHUNK ** -0.5 * nrm(ks[8], (DEPTH, SGU_GROUPS, CHUNK, CHUNK)),
        "b_sgu": 1.0 + 0.02 * nrm(ks[9], (DEPTH, SGU_GROUPS, CHUNK)),
        "v_norm_w": 1.0 + 0.02 * nrm(ks[10], (DEPTH, SGU_GROUPS, SGU_GROUP_DIM)),
        "q_norm_w": 1.0 + 0.02 * nrm(ks[11], (DEPTH, HEAD_DIM)),
        "k_norm_w": 1.0 + 0.02 * nrm(ks[12], (DEPTH, HEAD_DIM)),
        "w_out": D_MIX ** -0.5 * nrm(ks[13], (DEPTH, D_MIX, D_MODEL)),
    }


def reference(x, c, ctx, c_ctx, norm_w, w_mod, b_mod, w_in, w_sgu, b_sgu, v_norm_w,
              q_norm_w, k_norm_w, w_out):
    n_lat = x.shape[1]
    cos, sin = axial_rope_tables(n_lat, x.dtype)
    xc = ctx
    for layer in range(DEPTH):
        last = layer == DEPTH - 1
        shift, scale, gate = modulation(c, w_mod[layer], b_mod[layer])
        shift_c, scale_c, gate_c = modulation(c_ctx, w_mod[layer], b_mod[layer])
        h = rms_norm(x, norm_w[layer]) * (1.0 + scale) + shift
        hc = rms_norm(xc, norm_w[layer]) * (1.0 + scale_c) + shift_c

        if last:
            proj_kv_c = hc @ w_in[layer][:, KV_START:KV_END]
            k_c, v_c = jnp.split(proj_kv_c, 2, axis=-1)
            kc, vc = kv_heads(k_c, v_c, k_norm_w[layer])
        else:
            proj_c = hc @ w_in[layer]
            u_c, v_c_sgu, za_c, q_c, k_c, v_c, zb_c = jnp.split(proj_c, SPLIT_POINTS, axis=-1)
            kc, vc = kv_heads(k_c, v_c, k_norm_w[layer])
            qc = rms_norm(split_heads(q_c, N_Q_HEADS), q_norm_w[layer])
            attn_c = blocked_gqa(qc, kc, vc)
            sgu_c = chunk_sgu(u_c, v_c_sgu, w_sgu[layer], b_sgu[layer], v_norm_w[layer])
            y_c = jnp.concatenate([sgu_c * jax.nn.silu(za_c),
                                   merge_heads(attn_c) * jax.nn.silu(zb_c)], axis=-1) @ w_out[layer]

        proj = h @ w_in[layer]
        u, v_sgu, za, q, k, v, zb = jnp.split(proj, SPLIT_POINTS, axis=-1)
        q = apply_axial_rope(rms_norm(split_heads(q, N_Q_HEADS), q_norm_w[layer]), cos, sin)
        k, v = kv_heads(k, v, k_norm_w[layer])
        k = apply_axial_rope(k, cos, sin)
        k_all = jnp.concatenate([kc, k], axis=2)
        v_all = jnp.concatenate([vc, v], axis=2)
        attn = blocked_gqa(q, k_all, v_all)
        sgu = chunk_sgu(u, v_sgu, w_sgu[layer], b_sgu[layer], v_norm_w[layer])
        y = jnp.concatenate([sgu * jax.nn.silu(za),
                             merge_heads(attn) * jax.nn.silu(zb)], axis=-1) @ w_out[layer]
        x = x + gate * y
        if not last:
            xc = xc + gate_c * y_c
    return x
```

```python
import functools
import math

import jax
import jax.numpy as jnp
import numpy as np
from jax import lax
from jax.experimental import pallas as pl
from jax.experimental.pallas import tpu as pltpu

F32 = jnp.float32
BF16 = jnp.bfloat16

GRID_W = 64
CHUNK = 128
HEAD_DIM = 128
N_KV_HEADS = 2
SGU_GROUPS = 8
ROPE_THETA = 10000.0
EPS = 1e-6

LANES = 128
COL_TILE = 4 * LANES
VMEM_LIMIT = 48 * 1024 * 1024
Q_SCALE = HEAD_DIM ** -0.5 * math.log2(math.e)


def _params(*sem):
    return pltpu.CompilerParams(dimension_semantics=sem, vmem_limit_bytes=VMEM_LIMIT)


def _rms(t, w):
    return t * lax.rsqrt(jnp.mean(t * t, axis=-1, keepdims=True) + EPS) * w


def _silu(t):
    return t * jax.nn.sigmoid(t)


def _rope(t, cos, sin_lo, sin_hi):
    return t * cos + pltpu.roll(t, LANES - 32, 1) * sin_lo + pltpu.roll(t, 32, 1) * sin_hi


def _mod_kernel(cond_ref, w_ref, b_ref, o_ref):
    a = _silu(cond_ref[...]).astype(BF16)
    o_ref[0] = jnp.dot(a, w_ref[0].astype(BF16), preferred_element_type=F32) + b_ref[0]


def _modulation(cond, w_mod, b_mod, tn=1024):
    depth, d, n = w_mod.shape
    rows = cond.shape[0]
    return pl.pallas_call(
        _mod_kernel,
        out_shape=jax.ShapeDtypeStruct((depth, rows, n), F32),
        grid=(depth, n // tn),
        in_specs=[
            pl.BlockSpec((rows, d), lambda l, j: (0, 0)),
            pl.BlockSpec((1, d, tn), lambda l, j: (l, 0, j)),
            pl.BlockSpec((1, 1, tn), lambda l, j: (l, 0, j)),
        ],
        out_specs=pl.BlockSpec((1, rows, tn), lambda l, j: (l, 0, j)),
        compiler_params=_params("arbitrary", "arbitrary"),
        name="modulation",
    )(cond, w_mod, b_mod.reshape(depth, 1, n))


def _inproj_kernel(*refs, tm, j0, rope, kv_only):
    it = iter(refs)
    x_ref, nw_ref, scale_ref, shift_ref, w_ref, knw_ref = (next(it) for _ in range(6))
    if not kv_only:
        wsgu_ref, bsgu_ref, vnw_ref, qnw_ref = (next(it) for _ in range(4))
    if rope:
        cos_ref, slo_ref, shi_ref = (next(it) for _ in range(3))
    if not kv_only:
        sgu_out, zb_out, q_out = (next(it) for _ in range(3))
    k_out, vt_out, h_scr = (next(it) for _ in range(3))

    j = pl.program_id(1) + j0

    @pl.when(pl.program_id(1) == 0)
    def _():
        h = _rms(x_ref[...], nw_ref[...]) * (1.0 + scale_ref[0]) + shift_ref[0]
        h_scr[...] = h.astype(BF16)

    def project():
        return jnp.dot(h_scr[...], w_ref[...], preferred_element_type=F32)

    def maybe_rope(t):
        if rope:
            return _rope(t, cos_ref[...], slo_ref[...], shi_ref[...])
        return t

    def sgu_tile():
        acc = project()
        gu = jax.nn.gelu(acc[:, 0:LANES])
        vn = _rms(jax.nn.gelu(acc[:, LANES:2 * LANES]), vnw_ref[0]).astype(BF16)
        nc = tm // CHUNK
        v_cat = jnp.concatenate([vn[c * CHUNK:(c + 1) * CHUNK] for c in range(nc)], axis=1)
        s_cat = jnp.dot(wsgu_ref[0], v_cat, preferred_element_type=F32) + bsgu_ref[0]
        s = jnp.concatenate([s_cat[:, c * CHUNK:(c + 1) * CHUNK] for c in range(nc)], axis=0)
        sgu_out[...] = (gu * s * _silu(acc[:, 2 * LANES:3 * LANES])).astype(BF16)
        zb_out[...] = _silu(acc[:, 3 * LANES:4 * LANES]).astype(BF16)

    def q_tile():
        acc = project()
        for hh in range(COL_TILE // HEAD_DIM):
            sl = slice(hh * HEAD_DIM, (hh + 1) * HEAD_DIM)
            qn = maybe_rope(_rms(acc[:, sl], qnw_ref[...]))
            q_out[:, sl] = (qn * Q_SCALE).astype(BF16)

    def kv_tile():
        acc = project()
        for hh in range(N_KV_HEADS):
            sl = slice(hh * HEAD_DIM, (hh + 1) * HEAD_DIM)
            k_out[:, sl] = maybe_rope(_rms(acc[:, sl], knw_ref[...])).astype(BF16)
            v = acc[:, (N_KV_HEADS + hh) * HEAD_DIM:(N_KV_HEADS + hh + 1) * HEAD_DIM]
            vt_out[0, hh] = v.T.astype(BF16)

    if kv_only:
        kv_tile()
    else:
        pl.when(j < SGU_GROUPS)(sgu_tile)
        pl.when(jnp.logical_and(j >= SGU_GROUPS, j < SGU_GROUPS + 2))(q_tile)
        pl.when(j == SGU_GROUPS + 2)(kv_tile)


def _input_projection(x2d, seq, norm_w, scale, shift, w_in_p, k_norm_w, sgu_params, q_norm_w,
                      rope_tables, *, tm, kv_only):
    m, d = x2d.shape
    tpb = seq // tm
    nb = scale.shape[0]
    n_tiles = w_in_p.shape[1] // COL_TILE
    j0, nj = (n_tiles - 1, 1) if kv_only else (0, n_tiles)
    rope = rope_tables is not None
    g = SGU_GROUPS

    def mod_idx(i, j):
        return ((i // tpb) if nb > 1 else 0, 0, 0)

    const2 = lambda i, j: (0, 0)
    in_specs = [
        pl.BlockSpec((tm, d), lambda i, j: (i, 0)),
        pl.BlockSpec((1, d), const2),
        pl.BlockSpec((1, 1, d), mod_idx),
        pl.BlockSpec((1, 1, d), mod_idx),
        pl.BlockSpec((d, COL_TILE), lambda i, j: (0, j + j0)),
        pl.BlockSpec((1, HEAD_DIM), const2),
    ]
    args = [x2d, norm_w.reshape(1, d), scale, shift, w_in_p, k_norm_w.reshape(1, HEAD_DIM)]
    if not kv_only:
        w_sgu, b_sgu, v_norm_w = sgu_params
        grp = lambda i, j: (jnp.minimum(j, g - 1), 0, 0)
        in_specs += [
            pl.BlockSpec((1, CHUNK, CHUNK), grp),
            pl.BlockSpec((1, CHUNK, 1), grp),
            pl.BlockSpec((1, 1, LANES), grp),
            pl.BlockSpec((1, HEAD_DIM), const2),
        ]
        args += [w_sgu.astype(BF16), b_sgu.reshape(g, CHUNK, 1), v_norm_w.reshape(g, 1, LANES),
                 q_norm_w.reshape(1, HEAD_DIM)]
    if rope:
        in_specs += [pl.BlockSpec((tm, HEAD_DIM), lambda i, j: (i % tpb, 0))] * 3
        args += list(rope_tables)

    out_shape, out_specs = [], []
    if not kv_only:
        grp2 = lambda i, j: (i, jnp.minimum(j, g - 1))
        out_shape += [jax.ShapeDtypeStruct((m, g * LANES), BF16)] * 2
        out_specs += [pl.BlockSpec((tm, LANES), grp2)] * 2
        out_shape.append(jax.ShapeDtypeStruct((m, 2 * COL_TILE), BF16))
        out_specs.append(pl.BlockSpec((tm, COL_TILE), lambda i, j: (i, jnp.clip(j - g, 0, 1))))
    out_shape += [jax.ShapeDtypeStruct((m, N_KV_HEADS * HEAD_DIM), BF16),
                  jax.ShapeDtypeStruct((m // seq, N_KV_HEADS, HEAD_DIM, seq), BF16)]
    out_specs += [pl.BlockSpec((tm, N_KV_HEADS * HEAD_DIM), lambda i, j: (i, 0)),
                  pl.BlockSpec((1, N_KV_HEADS, HEAD_DIM, tm), lambda i, j: (i // tpb, 0, 0, i % tpb))]

    return pl.pallas_call(
        functools.partial(_inproj_kernel, tm=tm, j0=j0, rope=rope, kv_only=kv_only),
        out_shape=out_shape,
        grid=(m // tm, nj),
        in_specs=in_specs,
        out_specs=out_specs,
        scratch_shapes=[pltpu.VMEM((tm, d), BF16)],
        compiler_params=_params("arbitrary", "arbitrary"),
        name="input_projection_kv" if kv_only else "input_projection",
    )(*args)


def _attn_kernel(*refs, n_seg):
    q_ref = refs[0]
    k_refs = refs[1:1 + n_seg]
    vt_refs = refs[1 + n_seg:1 + 2 * n_seg]
    zb_ref, o_ref = refs[1 + 2 * n_seg:]
    q = q_ref[...]
    nt = (((1,), (1,)), ((), ()))
    s = [lax.dot_general(k[...], q, nt, preferred_element_type=F32) for k in k_refs]
    mx = functools.reduce(jnp.maximum, [jnp.max(t, axis=0, keepdims=True) for t in s])
    p = [jnp.exp2(t - mx) for t in s]
    den = functools.reduce(jnp.add, [jnp.sum(t, axis=0, keepdims=True) for t in p])
    o_t = functools.reduce(jnp.add, [
        jnp.dot(vt[0, 0], t.astype(BF16), preferred_element_type=F32) for vt, t in zip(vt_refs, p)])
    o = (o_t * (1.0 / den)).T
    o_ref[...] = (o * zb_ref[...].astype(F32)).astype(BF16)


def _attention(q, zb, segments, *, seq_q, tq):
    m, width = q.shape
    n_heads = width // HEAD_DIM
    rep = n_heads // N_KV_HEADS
    nqt = seq_q // tq
    batch = m // seq_q
    q_spec = pl.BlockSpec((tq, HEAD_DIM), lambda b, h, i: (b * nqt + i, h))
    k_specs, vt_specs, ks, vts = [], [], [], []
    for k, vt in segments:
        n = vt.shape[-1]
        k_specs.append(pl.BlockSpec((n, HEAD_DIM), lambda b, h, i: (b, h // rep)))
        vt_specs.append(pl.BlockSpec((1, 1, HEAD_DIM, n), lambda b, h, i: (b, h // rep, 0, 0)))
        ks.append(k)
        vts.append(vt)
    return pl.pallas_call(
        functools.partial(_attn_kernel, n_seg=len(segments)),
        out_shape=jax.ShapeDtypeStruct((m, width), BF16),
        grid=(batch, n_heads, nqt),
        in_specs=[q_spec] + k_specs + vt_specs + [q_spec],
        out_specs=q_spec,
        compiler_params=_params("arbitrary", "arbitrary", "arbitrary"),
        name="attention",
    )(q, *ks, *vts, zb)


def _outproj_kernel(a_ref, b_ref, w_ref, x_ref, g_ref, o_ref):
    ka = a_ref.shape[1]
    y = jnp.dot(a_ref[...], w_ref[0:ka, :], preferred_element_type=F32)
    y += jnp.dot(b_ref[...], w_ref[ka:, :], preferred_element_type=F32)
    o_ref[...] = x_ref[...] + g_ref[0] * y


def _output_projection(a, b, w_out, x2d, gate, seq, *, tm, tn=512):
    m, d = x2d.shape
    tpb = seq // tm
    nb = gate.shape[0]
    return pl.pallas_call(
        _outproj_kernel,
        out_shape=jax.ShapeDtypeStruct((m, d), F32),
        grid=(m // tm, d // tn),
        in_specs=[
            pl.BlockSpec((tm, a.shape[1]), lambda i, j: (i, 0)),
            pl.BlockSpec((tm, b.shape[1]), lambda i, j: (i, 0)),
            pl.BlockSpec((w_out.shape[0], tn), lambda i, j: (0, j)),
            pl.BlockSpec((tm, tn), lambda i, j: (i, j)),
            pl.BlockSpec((1, 1, tn), lambda i, j: ((i // tpb) if nb > 1 else 0, 0, j)),
        ],
        out_specs=pl.BlockSpec((tm, tn), lambda i, j: (i, j)),
        compiler_params=_params("arbitrary", "arbitrary"),
        name="output_projection",
    )(a, b, w_out, x2d, gate)


def _rope_tables(n_tokens):
    rows = n_tokens // GRID_W
    row_id = jnp.broadcast_to(jnp.arange(rows)[:, None], (rows, GRID_W)).reshape(-1)
    col_id = jnp.broadcast_to(jnp.arange(GRID_W)[None, :], (rows, GRID_W)).reshape(-1)
    axis_dim = HEAD_DIM // 2
    inv_freq = ROPE_THETA ** (-jnp.arange(0, axis_dim, 2, dtype=F32) / axis_dim)
    ang_r = row_id.astype(F32)[:, None] * inv_freq[None, :]
    ang_c = col_id.astype(F32)[:, None] * inv_freq[None, :]
    ang = jnp.concatenate([ang_r, ang_r, ang_c, ang_c], axis=-1)
    cos, sin = jnp.cos(ang), jnp.sin(ang)
    first_half = (jnp.arange(HEAD_DIM) % axis_dim) < axis_dim // 2
    return cos, jnp.where(first_half, -sin, 0.0), jnp.where(first_half, 0.0, sin)


def _permute_w_in(w, sgu_width, attn_width):
    zb0 = 3 * sgu_width + attn_width + 2 * N_KV_HEADS * HEAD_DIM
    cols = []
    for g in range(SGU_GROUPS):
        for base in (0, sgu_width, 2 * sgu_width, zb0):
            cols.append(w[:, base + g * LANES: base + (g + 1) * LANES])
    cols.append(w[:, 3 * sgu_width:zb0])
    return jnp.concatenate(cols, axis=1).astype(BF16)


def kernel(x, c, ctx, c_ctx, norm_w, w_mod, b_mod, w_in, w_sgu, b_sgu, v_norm_w, q_norm_w, k_norm_w, w_out):
    batch, seq, d = x.shape
    ctx_len = ctx.shape[1]
    depth = norm_w.shape[0]
    sgu_width = SGU_GROUPS * v_norm_w.shape[-1]
    attn_width = w_out.shape[1] - sgu_width
    assert sgu_width == SGU_GROUPS * LANES and attn_width == sgu_width
    assert w_in.shape[2] == 3 * sgu_width + 2 * attn_width + 2 * N_KV_HEADS * HEAD_DIM

    rows = 8 * pl.cdiv(batch + 1, 8)
    cond = jnp.zeros((rows, d), F32).at[:batch].set(c).at[batch].set(c_ctx)
    mod = _modulation(cond, w_mod, b_mod)

    tables = _rope_tables(seq)
    xl = x.reshape(batch * seq, d)
    xc = ctx.reshape(batch * ctx_len, d)
    tm_lat, tm_ctx = 512, ctx_len

    for layer in range(depth):
        last = layer == depth - 1
        shift, scale, gate = (mod[layer, :, None, i * d:(i + 1) * d] for i in range(3))
        w_in_p = _permute_w_in(w_in[layer], sgu_width, attn_width)
        w_out_b = w_out[layer].astype(BF16)
        sgu_params = (w_sgu[layer], b_sgu[layer], v_norm_w[layer])
        lat_mod = lambda t: t[:batch]
        ctx_mod = lambda t: t[batch:batch + 1]

        ctx_out = _input_projection(
            xc, ctx_len, norm_w[layer], ctx_mod(scale), ctx_mod(shift), w_in_p, k_norm_w[layer],
            sgu_params, q_norm_w[layer], None, tm=tm_ctx, kv_only=last)
        kc, vtc = ctx_out[-2:]
        if not last:
            sgu_c, zb_c, q_c = ctx_out[:3]
            attn_c = _attention(q_c, zb_c, [(kc, vtc)], seq_q=ctx_len, tq=ctx_len)
            xc_next = _output_projection(sgu_c, attn_c, w_out_b, xc, ctx_mod(gate), ctx_len, tm=tm_ctx)

        sgu_l, zb_l, q_l, kl, vtl = _input_projection(
            xl, seq, norm_w[layer], lat_mod(scale), lat_mod(shift), w_in_p, k_norm_w[layer],
            sgu_params, q_norm_w[layer], tables, tm=tm_lat, kv_only=False)
        attn_l = _attention(q_l, zb_l, [(kc, vtc), (kl, vtl)], seq_q=seq, tq=256)
        xl = _output_projection(sgu_l, attn_l, w_out_b, xl, lat_mod(gate), seq, tm=tm_lat)
        if not last:
            xc = xc_next
    return xl.reshape(batch, seq, d)
```

```python
import functools
import itertools
import math

import jax
import jax.numpy as jnp
from jax import lax
from jax.experimental import pallas as pl
from jax.experimental.pallas import tpu as pltpu

F32 = jnp.float32
BF16 = jnp.bfloat16

GRID_W = 64
CHUNK = 128
HEAD_DIM = 128
N_KV_HEADS = 2
SGU_GROUPS = 8
ROPE_THETA = 10000.0
EPS = 1e-6

LANES = 128
SUBLANES = 8
COL_TILE = 4 * LANES
K_CHUNK = 512
NORM_ROWS = 32
VMEM_LIMIT = 56 * 1024 * 1024
ATTN_HEADS_PER_STEP = 2
QK_ROWS = 512
SM_ROWS = 64
Q_SCALE = HEAD_DIM ** -0.5 * math.log2(math.e)


def _params(*sem):
    return pltpu.CompilerParams(dimension_semantics=sem, vmem_limit_bytes=VMEM_LIMIT)


def _rms(t, w):
    return t * lax.rsqrt(jnp.mean(t * t, axis=-1, keepdims=True) + EPS) * w


def _silu(t):
    return t * jax.nn.sigmoid(t)


def _rope(t, cos, sin_signed):
    return t * cos + pltpu.roll(t, HEAD_DIM // 2, 1) * sin_signed


def _fold_rows(x, op):
    parts = [x[i:i + SUBLANES] for i in range(0, x.shape[0], SUBLANES)]
    while len(parts) > 1:
        parts = [op(a, b) for a, b in zip(parts[0::2], parts[1::2])] + ([parts[-1]] if len(parts) % 2 else [])
    return parts[0]


def _mod_kernel(cond_ref, w_ref, b_ref, o_ref):
    a = _silu(cond_ref[...]).astype(BF16)
    o_ref[0] = jnp.dot(a, w_ref[0].astype(BF16), preferred_element_type=F32) + b_ref[0]


def _modulation(cond, w_mod, b_mod, tn=1024):
    depth, d, n = w_mod.shape
    rows = cond.shape[0]
    return pl.pallas_call(
        _mod_kernel,
        out_shape=jax.ShapeDtypeStruct((depth, rows, n), F32),
        grid=(depth, n // tn),
        in_specs=[
            pl.BlockSpec((rows, d), lambda l, j: (0, 0)),
            pl.BlockSpec((1, d, tn), lambda l, j: (l, 0, j)),
            pl.BlockSpec((1, 1, tn), lambda l, j: (l, 0, j)),
        ],
        out_specs=pl.BlockSpec((1, rows, tn), lambda l, j: (l, 0, j)),
        compiler_params=_params("arbitrary", "arbitrary"),
        name="modulation",
    )(cond, w_mod, b_mod.reshape(depth, 1, n))


def _inproj_kernel(*refs, tm, rope, kv_only):
    it = iter(refs)
    x_ref, nw_ref, scale_ref, shift_ref, w_ref, knw_ref = (next(it) for _ in range(6))
    if not kv_only:
        wsgu_ref, bsgu_ref, vnw_ref, qnw_ref = (next(it) for _ in range(4))
    if rope:
        cos_ref, sin_ref = (next(it) for _ in range(2))
    if not kv_only:
        sgu_out, zb_out, q_out = (next(it) for _ in range(3))
    k_out, vt_out, h_scr, acc_scr = (next(it) for _ in range(4))

    d = h_scr.shape[1]
    n_tiles = w_ref.shape[1] // COL_TILE
    heads_per_tile = COL_TILE // HEAD_DIM

    nw, gain, shift = nw_ref[...], 1.0 + scale_ref[0], shift_ref[0]
    for r0 in range(0, tm, NORM_ROWS):
        rows = slice(r0, r0 + NORM_ROWS)
        h_scr[rows, :] = (_rms(x_ref[rows, :], nw) * gain + shift).astype(BF16)

    def maybe_rope(t, rows):
        return _rope(t, cos_ref[rows, :], sin_ref[rows, :]) if rope else t

    def matmul_pieces(n, slot):
        cols = slice(n * COL_TILE, (n + 1) * COL_TILE)
        acc = []

        def piece(k0):
            part = jnp.dot(h_scr[:, k0:k0 + K_CHUNK], w_ref[k0:k0 + K_CHUNK, cols], preferred_element_type=F32)
            acc[:] = [part if not acc else acc[0] + part]
            if k0 + K_CHUNK == d:
                acc_scr[slot] = acc[0]
        return [functools.partial(piece, k0) for k0 in range(0, d, K_CHUNK)]

    def sgu_piece(g, slot, r0):
        rows = slice(r0, r0 + CHUNK)
        a = acc_scr[slot, rows, :]
        gu = jax.nn.gelu(a[:, 0:LANES])
        vn = _rms(jax.nn.gelu(a[:, LANES:2 * LANES]), vnw_ref[g]).astype(BF16)
        s = jnp.dot(wsgu_ref[g], vn, preferred_element_type=F32) + bsgu_ref[g]
        cols = slice(g * LANES, (g + 1) * LANES)
        sgu_out[rows, cols] = (gu * s * _silu(a[:, 2 * LANES:3 * LANES])).astype(BF16)
        zb_out[rows, cols] = _silu(a[:, 3 * LANES:4 * LANES]).astype(BF16)

    def q_piece(qt, slot, r0):
        rows = slice(r0, r0 + CHUNK)
        a = acc_scr[slot, rows, :]
        for hh in range(heads_per_tile):
            qn = maybe_rope(_rms(a[:, hh * HEAD_DIM:(hh + 1) * HEAD_DIM], qnw_ref[...]), rows)
            c0 = (qt * heads_per_tile + hh) * HEAD_DIM
            q_out[rows, c0:c0 + HEAD_DIM] = (qn * Q_SCALE).astype(BF16)

    def kv_piece(slot, r0):
        rows = slice(r0, r0 + CHUNK)
        a = acc_scr[slot, rows, :]
        for hh in range(N_KV_HEADS):
            cols = slice(hh * HEAD_DIM, (hh + 1) * HEAD_DIM)
            k_out[rows, cols] = maybe_rope(_rms(a[:, cols], knw_ref[...]), rows).astype(BF16)
            v = a[:, (N_KV_HEADS + hh) * HEAD_DIM:(N_KV_HEADS + hh + 1) * HEAD_DIM]
            vt_out[0, hh, :, rows] = v.T.astype(BF16)

    def epilogue_pieces(n, slot):
        row_starts = range(0, tm, CHUNK)
        if kv_only or n == n_tiles - 1:
            return [functools.partial(kv_piece, slot, r0) for r0 in row_starts]
        if n < SGU_GROUPS:
            return [functools.partial(sgu_piece, n, slot, r0) for r0 in row_starts]
        return [functools.partial(q_piece, n - SGU_GROUPS, slot, r0) for r0 in row_starts]

    pending = []
    for n in range(n_tiles):
        slot = n % 2
        for mm, ep in itertools.zip_longest(matmul_pieces(n, slot), pending):
            if mm is not None:
                mm()
            if ep is not None:
                ep()
        pending = epilogue_pieces(n, slot)
    for ep in pending:
        ep()


def _input_projection(x2d, seq, norm_w, scale, shift, w_in_p, k_norm_w, sgu_params, q_norm_w,
                      rope_tables, *, tm, kv_only):
    m, d = x2d.shape
    tpb = seq // tm
    nb = scale.shape[0]
    n_tiles = w_in_p.shape[1] // COL_TILE
    rope = rope_tables is not None
    g = SGU_GROUPS
    resident = pl.Buffered(1)

    def mod_idx(i):
        return ((i // tpb) if nb > 1 else 0, 0, 0)

    const2 = lambda i: (0, 0)
    const3 = lambda i: (0, 0, 0)
    if kv_only:
        w_spec = pl.BlockSpec((d, COL_TILE), lambda i: (0, n_tiles - 1), pipeline_mode=resident)
    else:
        w_spec = pl.BlockSpec((d, n_tiles * COL_TILE), const2, pipeline_mode=resident)
    in_specs = [
        pl.BlockSpec((tm, d), lambda i: (i, 0)),
        pl.BlockSpec((1, d), const2),
        pl.BlockSpec((1, 1, d), mod_idx),
        pl.BlockSpec((1, 1, d), mod_idx),
        w_spec,
        pl.BlockSpec((1, HEAD_DIM), const2),
    ]
    args = [x2d, norm_w.reshape(1, d), scale, shift, w_in_p, k_norm_w.reshape(1, HEAD_DIM)]
    if not kv_only:
        w_sgu, b_sgu, v_norm_w = sgu_params
        in_specs += [
            pl.BlockSpec((g, CHUNK, CHUNK), const3),
            pl.BlockSpec((g, CHUNK, 1), const3),
            pl.BlockSpec((g, 1, LANES), const3),
            pl.BlockSpec((1, HEAD_DIM), const2),
        ]
        args += [w_sgu.astype(BF16), b_sgu.reshape(g, CHUNK, 1), v_norm_w.reshape(g, 1, LANES),
                 q_norm_w.reshape(1, HEAD_DIM)]
    if rope:
        in_specs += [pl.BlockSpec((tm, HEAD_DIM), lambda i: (i % tpb, 0))] * 2
        args += list(rope_tables)

    out_shape, out_specs = [], []
    if not kv_only:
        out_shape += [jax.ShapeDtypeStruct((m, g * LANES), BF16)] * 2
        out_specs += [pl.BlockSpec((tm, g * LANES), lambda i: (i, 0))] * 2
        q_width = (n_tiles - g - 1) * COL_TILE
        out_shape.append(jax.ShapeDtypeStruct((m, q_width), BF16))
        out_specs.append(pl.BlockSpec((tm, q_width), lambda i: (i, 0)))
    out_shape += [jax.ShapeDtypeStruct((m, N_KV_HEADS * HEAD_DIM), BF16),
                  jax.ShapeDtypeStruct((m // seq, N_KV_HEADS, HEAD_DIM, seq), BF16)]
    out_specs += [pl.BlockSpec((tm, N_KV_HEADS * HEAD_DIM), lambda i: (i, 0)),
                  pl.BlockSpec((1, N_KV_HEADS, HEAD_DIM, tm), lambda i: (i // tpb, 0, 0, i % tpb))]

    return pl.pallas_call(
        functools.partial(_inproj_kernel, tm=tm, rope=rope, kv_only=kv_only),
        out_shape=out_shape,
        grid=(m // tm,),
        in_specs=in_specs,
        out_specs=out_specs,
        scratch_shapes=[pltpu.VMEM((tm, d), BF16), pltpu.VMEM((2, tm, COL_TILE), F32)],
        compiler_params=_params("arbitrary"),
        name="input_projection_kv" if kv_only else "input_projection",
    )(*args)


def _attn_kernel(*refs, n_seg, hp):
    q_ref = refs[0]
    k_refs = refs[1:1 + n_seg]
    vt_refs = refs[1 + n_seg:1 + 2 * n_seg]
    zb_ref, o_ref = refs[1 + 2 * n_seg:]
    nt = (((1,), (1,)), ((), ()))
    tq = q_ref.shape[0]
    heads = [slice(hh * HEAD_DIM, (hh + 1) * HEAD_DIM) for hh in range(hp)]
    q = jnp.concatenate([q_ref[:, sl] for sl in heads], axis=0)
    s = [lax.dot_general(k[...], q, nt, preferred_element_type=F32) for k in k_refs]
    mx = functools.reduce(jnp.maximum, [jnp.max(t, axis=0, keepdims=True) for t in s])
    p = [jnp.exp2(t - mx) for t in s]
    den = functools.reduce(jnp.add, [jnp.sum(t, axis=0, keepdims=True) for t in p])
    o_t = functools.reduce(jnp.add, [
        jnp.dot(vt[0, 0], t.astype(BF16), preferred_element_type=F32) for vt, t in zip(vt_refs, p)])
    o_t = o_t * (1.0 / den)
    for hh, sl in enumerate(heads):
        o = o_t[:, hh * tq:(hh + 1) * tq].T
        o_ref[:, sl] = (o * zb_ref[:, sl].astype(F32)).astype(BF16)


def _attention(q, zb, segments, *, seq_q, tq, hp):
    m, width = q.shape
    n_heads = width // HEAD_DIM
    rep = n_heads // N_KV_HEADS
    nqt = seq_q // tq
    batch = m // seq_q
    assert rep % hp == 0
    q_spec = pl.BlockSpec((tq, hp * HEAD_DIM), lambda b, h, i: (b * nqt + i, h))
    k_specs, vt_specs, ks, vts = [], [], [], []
    for k, vt in segments:
        n = vt.shape[-1]
        k_specs.append(pl.BlockSpec((n, HEAD_DIM), lambda b, h, i: (b, (h * hp) // rep)))
        vt_specs.append(pl.BlockSpec((1, 1, HEAD_DIM, n), lambda b, h, i: (b, (h * hp) // rep, 0, 0)))
        ks.append(k)
        vts.append(vt)
    return pl.pallas_call(
        functools.partial(_attn_kernel, n_seg=len(segments), hp=hp),
        out_shape=jax.ShapeDtypeStruct((m, width), BF16),
        grid=(batch, n_heads // hp, nqt),
        in_specs=[q_spec] + k_specs + vt_specs + [q_spec],
        out_specs=q_spec,
        compiler_params=_params("arbitrary", "arbitrary", "arbitrary"),
        name="attention",
    )(q, *ks, *vts, zb)


def _attn_pipelined_kernel(q_ref, kc_ref, kl_ref, vtc_ref, vtl_ref, zb_ref, o_ref,
                           s_even, mx_even, s_odd, mx_odd, p_scr, o_scr, *, hp):
    t = pl.program_id(0)
    nt = (((1,), (1,)), ((), ()))
    tq = q_ref.shape[0]
    n_ctx = kc_ref.shape[0]
    heads = [slice(hh * HEAD_DIM, (hh + 1) * HEAD_DIM) for hh in range(hp)]

    @pl.when(t == 0)
    def _():
        s_odd[...] = jnp.zeros_like(s_odd)
        mx_odd[...] = jnp.zeros_like(mx_odd)

    def step(s_w, mx_w, s_r, mx_r):
        q = jnp.concatenate([q_ref[:, sl] for sl in heads], axis=0)
        mx = mx_r[...]
        mx8 = den8 = None
        chunks = [(0, kc_ref[...], vtc_ref[0, 0])] + [
            (n_ctx + r0, kl_ref[r0:r0 + QK_ROWS, :], vtl_ref[0, 0, :, r0:r0 + QK_ROWS])
            for r0 in range(0, kl_ref.shape[0], QK_ROWS)]
        for ci, (r0, k, vt) in enumerate(chunks):
            rows = k.shape[0]
            sc = lax.dot_general(k, q, nt, preferred_element_type=F32)
            s_w[r0:r0 + rows, :] = sc
            m8 = _fold_rows(sc, jnp.maximum)
            mx8 = m8 if mx8 is None else jnp.maximum(mx8, m8)
            for r1 in range(r0, r0 + rows, SM_ROWS):
                pc = jnp.exp2(s_r[r1:r1 + SM_ROWS, :] - mx)
                d8 = _fold_rows(pc, jnp.add)
                den8 = d8 if den8 is None else den8 + d8
                p_scr[r1:r1 + SM_ROWS, :] = pc.astype(BF16)
            pv = jnp.dot(vt, p_scr[r0:r0 + rows, :], preferred_element_type=F32)
            if ci == 0:
                o_scr[...] = pv
            else:
                o_scr[...] += pv
        mx_w[...] = jnp.max(mx8, axis=0, keepdims=True)
        den = jnp.sum(den8, axis=0, keepdims=True)
        o_t = o_scr[...] * (1.0 / den)
        for hh, sl in enumerate(heads):
            o = o_t[:, hh * tq:(hh + 1) * tq].T
            o_ref[:, sl] = (o * zb_ref[:, sl].astype(F32)).astype(BF16)

    pl.when(t % 2 == 0)(lambda: step(s_even, mx_even, s_odd, mx_odd))
    pl.when(t % 2 == 1)(lambda: step(s_odd, mx_odd, s_even, mx_even))


def _attention_pipelined(q, zb, ctx_seg, lat_seg, *, seq_q, tq, hp):
    m, width = q.shape
    n_heads = width // HEAD_DIM
    rep = n_heads // N_KV_HEADS
    assert rep % hp == 0
    n_hg = n_heads // hp
    nqt = seq_q // tq
    n_tiles = (m // seq_q) * n_hg * nqt
    (kc, vtc), (kl, vtl) = ctx_seg, lat_seg
    n_ctx, n_lat = vtc.shape[-1], vtl.shape[-1]

    def decode(tile):
        return tile // (n_hg * nqt), (tile // nqt) % n_hg, tile % nqt

    def cur(t):
        return decode(jnp.minimum(t, n_tiles - 1))

    def prev(t):
        return decode(jnp.maximum(t - 1, 0))

    def q_idx(b, hg, qi):
        return (b * nqt + qi, hg)

    q_block = (tq, hp * HEAD_DIM)
    n_cols = hp * tq
    return pl.pallas_call(
        functools.partial(_attn_pipelined_kernel, hp=hp),
        out_shape=jax.ShapeDtypeStruct((m, width), BF16),
        grid=(n_tiles + 1,),
        in_specs=[
            pl.BlockSpec(q_block, lambda t: q_idx(*cur(t))),
            pl.BlockSpec((n_ctx, HEAD_DIM), lambda t: (cur(t)[0], (cur(t)[1] * hp) // rep)),
            pl.BlockSpec((n_lat, HEAD_DIM), lambda t: (cur(t)[0], (cur(t)[1] * hp) // rep)),
            pl.BlockSpec((1, 1, HEAD_DIM, n_ctx), lambda t: (prev(t)[0], (prev(t)[1] * hp) // rep, 0, 0)),
            pl.BlockSpec((1, 1, HEAD_DIM, n_lat), lambda t: (prev(t)[0], (prev(t)[1] * hp) // rep, 0, 0)),
            pl.BlockSpec(q_block, lambda t: q_idx(*prev(t))),
        ],
        out_specs=pl.BlockSpec(q_block, lambda t: q_idx(*prev(t))),
        scratch_shapes=[pltpu.VMEM((n_ctx + n_lat, n_cols), F32), pltpu.VMEM((1, n_cols), F32)] * 2
        + [pltpu.VMEM((n_ctx + n_lat, n_cols), BF16), pltpu.VMEM((HEAD_DIM, n_cols), F32)],
        compiler_params=_params("arbitrary"),
        name="attention_pipelined",
    )(q, kc, kl, vtc, vtl, zb)


def _outproj_kernel(a_ref, b_ref, w_ref, x_ref, g_ref, o_ref):
    ka = a_ref.shape[1]
    y = jnp.dot(a_ref[...], w_ref[0:ka, :], preferred_element_type=F32)
    y += jnp.dot(b_ref[...], w_ref[ka:, :], preferred_element_type=F32)
    o_ref[...] = x_ref[...] + g_ref[0] * y


def _output_projection(a, b, w_out, x2d, gate, seq, *, tm, tn=512):
    m, d = x2d.shape
    tpb = seq // tm
    nb = gate.shape[0]
    return pl.pallas_call(
        _outproj_kernel,
        out_shape=jax.ShapeDtypeStruct((m, d), F32),
        grid=(m // tm, d // tn),
        in_specs=[
            pl.BlockSpec((tm, a.shape[1]), lambda i, j: (i, 0)),
            pl.BlockSpec((tm, b.shape[1]), lambda i, j: (i, 0)),
            pl.BlockSpec((w_out.shape[0], tn), lambda i, j: (0, j)),
            pl.BlockSpec((tm, tn), lambda i, j: (i, j)),
            pl.BlockSpec((1, 1, tn), lambda i, j: ((i // tpb) if nb > 1 else 0, 0, j)),
        ],
        out_specs=pl.BlockSpec((tm, tn), lambda i, j: (i, j)),
        compiler_params=_params("arbitrary", "arbitrary"),
        name="output_projection",
    )(a, b, w_out, x2d, gate)


def _pair_dims(t):
    lead = t.shape[:-1]
    quarter = HEAD_DIM // 4
    t = t.reshape(*lead, -1, 2, 2, quarter)
    return jnp.swapaxes(t, -3, -2).reshape(*lead, -1)


def _rope_tables(n_tokens):
    rows = n_tokens // GRID_W
    row_id = jnp.broadcast_to(jnp.arange(rows)[:, None], (rows, GRID_W)).reshape(-1)
    col_id = jnp.broadcast_to(jnp.arange(GRID_W)[None, :], (rows, GRID_W)).reshape(-1)
    axis_dim = HEAD_DIM // 2
    inv_freq = ROPE_THETA ** (-jnp.arange(0, axis_dim, 2, dtype=F32) / axis_dim)
    ang_r = row_id.astype(F32)[:, None] * inv_freq[None, :]
    ang_c = col_id.astype(F32)[:, None] * inv_freq[None, :]
    ang = _pair_dims(jnp.concatenate([ang_r, ang_r, ang_c, ang_c], axis=-1))
    sign = jnp.where(jnp.arange(HEAD_DIM) < HEAD_DIM // 2, -1.0, 1.0)
    return jnp.cos(ang), jnp.sin(ang) * sign


def _permute_w_in(w, sgu_width, attn_width):
    kv_width = N_KV_HEADS * HEAD_DIM
    q0 = 3 * sgu_width
    k0 = q0 + attn_width
    zb0 = k0 + 2 * kv_width
    cols = []
    for g in range(SGU_GROUPS):
        for base in (0, sgu_width, 2 * sgu_width, zb0):
            cols.append(w[:, base + g * LANES: base + (g + 1) * LANES])
    cols += [_pair_dims(w[:, q0:k0]), _pair_dims(w[:, k0:k0 + kv_width]), w[:, k0 + kv_width:zb0]]
    return jnp.concatenate(cols, axis=1).astype(BF16)


def kernel(x, c, ctx, c_ctx, norm_w, w_mod, b_mod, w_in, w_sgu, b_sgu, v_norm_w, q_norm_w, k_norm_w, w_out):
    batch, seq, d = x.shape
    ctx_len = ctx.shape[1]
    depth = norm_w.shape[0]
    sgu_width = SGU_GROUPS * v_norm_w.shape[-1]
    attn_width = w_out.shape[1] - sgu_width
    assert sgu_width == SGU_GROUPS * LANES and attn_width == sgu_width
    assert w_in.shape[2] == 3 * sgu_width + 2 * attn_width + 2 * N_KV_HEADS * HEAD_DIM

    rows = 8 * pl.cdiv(batch + 1, 8)
    cond = jnp.zeros((rows, d), F32).at[:batch].set(c).at[batch].set(c_ctx)
    mod = _modulation(cond, w_mod, b_mod)

    tables = _rope_tables(seq)
    xl = x.reshape(batch * seq, d)
    xc = ctx.reshape(batch * ctx_len, d)
    tm_lat, tm_ctx = 512, ctx_len

    for layer in range(depth):
        last = layer == depth - 1
        shift, scale, gate = (mod[layer, :, None, i * d:(i + 1) * d] for i in range(3))
        w_in_p = _permute_w_in(w_in[layer], sgu_width, attn_width)
        w_out_b = w_out[layer].astype(BF16)
        sgu_params = (w_sgu[layer], b_sgu[layer], v_norm_w[layer])
        qnw, knw = _pair_dims(q_norm_w[layer]), _pair_dims(k_norm_w[layer])
        lat_mod = lambda t: t[:batch]
        ctx_mod = lambda t: t[batch:batch + 1]

        ctx_out = _input_projection(
            xc, ctx_len, norm_w[layer], ctx_mod(scale), ctx_mod(shift), w_in_p, knw,
            sgu_params, qnw, None, tm=tm_ctx, kv_only=last)
        kc, vtc = ctx_out[-2:]
        if not last:
            sgu_c, zb_c, q_c = ctx_out[:3]
            attn_c = _attention(q_c, zb_c, [(kc, vtc)], seq_q=ctx_len, tq=ctx_len, hp=ATTN_HEADS_PER_STEP)
            xc_next = _output_projection(sgu_c, attn_c, w_out_b, xc, ctx_mod(gate), ctx_len, tm=tm_ctx)

        sgu_l, zb_l, q_l, kl, vtl = _input_projection(
            xl, seq, norm_w[layer], lat_mod(scale), lat_mod(shift), w_in_p, knw,
            sgu_params, qnw, tables, tm=tm_lat, kv_only=False)
        attn_l = _attention_pipelined(q_l, zb_l, (kc, vtc), (kl, vtl), seq_q=seq, tq=256,
                                      hp=ATTN_HEADS_PER_STEP)
        xl = _output_projection(sgu_l, attn_l, w_out_b, xl, lat_mod(gate), seq, tm=tm_lat)
        if not last:
            xc = xc_next
    return xl.reshape(batch, seq, d)
```

```python
import functools
import itertools
import math

import jax
import jax.numpy as jnp
from jax import lax
from jax.experimental import pallas as pl
from jax.experimental.pallas import tpu as pltpu

F32 = jnp.float32
BF16 = jnp.bfloat16

GRID_W = 64
CHUNK = 128
HEAD_DIM = 128
N_KV_HEADS = 2
SGU_GROUPS = 8
ROPE_THETA = 10000.0
EPS = 1e-6

LANES = 128
SUBLANES = 8
COL_TILE = 4 * LANES
K_CHUNK = 512
NORM_ROWS = 32
VMEM_LIMIT = 56 * 1024 * 1024
ATTN_HEADS_PER_STEP = 2
QK_ROWS = 512
SM_ROWS = 64
Q_SCALE = HEAD_DIM ** -0.5 * math.log2(math.e)


def _params(*sem):
    return pltpu.CompilerParams(dimension_semantics=sem, vmem_limit_bytes=VMEM_LIMIT)


def _rms(t, w):
    return t * lax.rsqrt(jnp.mean(t * t, axis=-1, keepdims=True) + EPS) * w


def _silu(t):
    return t * jax.nn.sigmoid(t)


def _rope(t, cos, sin_signed):
    return t * cos + pltpu.roll(t, HEAD_DIM // 2, 1) * sin_signed


def _fold_rows(x, op):
    parts = [x[i:i + SUBLANES] for i in range(0, x.shape[0], SUBLANES)]
    while len(parts) > 1:
        parts = [op(a, b) for a, b in zip(parts[0::2], parts[1::2])] + ([parts[-1]] if len(parts) % 2 else [])
    return parts[0]


def _mod_kernel(cond_ref, w_ref, b_ref, o_ref):
    a = _silu(cond_ref[...]).astype(BF16)
    o_ref[0] = jnp.dot(a, w_ref[0].astype(BF16), preferred_element_type=F32) + b_ref[0]


def _modulation(cond, w_mod, b_mod, tn=1024):
    depth, d, n = w_mod.shape
    rows = cond.shape[0]
    return pl.pallas_call(
        _mod_kernel,
        out_shape=jax.ShapeDtypeStruct((depth, rows, n), F32),
        grid=(depth, n // tn),
        in_specs=[
            pl.BlockSpec((rows, d), lambda l, j: (0, 0)),
            pl.BlockSpec((1, d, tn), lambda l, j: (l, 0, j)),
            pl.BlockSpec((1, 1, tn), lambda l, j: (l, 0, j)),
        ],
        out_specs=pl.BlockSpec((1, rows, tn), lambda l, j: (l, 0, j)),
        compiler_params=_params("arbitrary", "arbitrary"),
        name="modulation",
    )(cond, w_mod, b_mod.reshape(depth, 1, n))


def _inproj_kernel(*refs, tm, rope, kv_only):
    it = iter(refs)
    x_ref, nw_ref, scale_ref, shift_ref, w_ref, knw_ref = (next(it) for _ in range(6))
    if not kv_only:
        wsgu_ref, bsgu_ref, vnw_ref, qnw_ref = (next(it) for _ in range(4))
    if rope:
        cos_ref, sin_ref = (next(it) for _ in range(2))
    if not kv_only:
        sgu_out, zb_out, q_out = (next(it) for _ in range(3))
    k_out, vt_out, h_scr, acc_scr = (next(it) for _ in range(4))

    d = h_scr.shape[1]
    n_tiles = w_ref.shape[1] // COL_TILE
    heads_per_tile = COL_TILE // HEAD_DIM

    nw, gain, shift = nw_ref[...], 1.0 + scale_ref[0, 0], shift_ref[0, 0]
    for r0 in range(0, tm, NORM_ROWS):
        rows = slice(r0, r0 + NORM_ROWS)
        h_scr[rows, :] = (_rms(x_ref[rows, :], nw) * gain + shift).astype(BF16)

    def maybe_rope(t, rows):
        return _rope(t, cos_ref[rows, :], sin_ref[rows, :]) if rope else t

    def matmul_pieces(n, slot):
        cols = slice(n * COL_TILE, (n + 1) * COL_TILE)
        acc = []

        def piece(k0):
            part = jnp.dot(h_scr[:, k0:k0 + K_CHUNK], w_ref[k0:k0 + K_CHUNK, cols], preferred_element_type=F32)
            acc[:] = [part if not acc else acc[0] + part]
            if k0 + K_CHUNK == d:
                acc_scr[slot] = acc[0]
        return [functools.partial(piece, k0) for k0 in range(0, d, K_CHUNK)]

    def sgu_piece(g, slot, r0):
        rows = slice(r0, r0 + CHUNK)
        a = acc_scr[slot, rows, :]
        gu = jax.nn.gelu(a[:, 0:LANES])
        vn = _rms(jax.nn.gelu(a[:, LANES:2 * LANES]), vnw_ref[g]).astype(BF16)
        s = jnp.dot(wsgu_ref[g], vn, preferred_element_type=F32) + bsgu_ref[g]
        cols = slice(g * LANES, (g + 1) * LANES)
        sgu_out[rows, cols] = (gu * s * _silu(a[:, 2 * LANES:3 * LANES])).astype(BF16)
        zb_out[rows, cols] = _silu(a[:, 3 * LANES:4 * LANES]).astype(BF16)

    def q_piece(qt, slot, r0):
        rows = slice(r0, r0 + CHUNK)
        a = acc_scr[slot, rows, :]
        for hh in range(heads_per_tile):
            qn = maybe_rope(_rms(a[:, hh * HEAD_DIM:(hh + 1) * HEAD_DIM], qnw_ref[...]), rows)
            c0 = (qt * heads_per_tile + hh) * HEAD_DIM
            q_out[rows, c0:c0 + HEAD_DIM] = (qn * Q_SCALE).astype(BF16)

    def kv_piece(slot, r0):
        rows = slice(r0, r0 + CHUNK)
        a = acc_scr[slot, rows, :]
        for hh in range(N_KV_HEADS):
            cols = slice(hh * HEAD_DIM, (hh + 1) * HEAD_DIM)
            k_out[rows, cols] = maybe_rope(_rms(a[:, cols], knw_ref[...]), rows).astype(BF16)
            v = a[:, (N_KV_HEADS + hh) * HEAD_DIM:(N_KV_HEADS + hh + 1) * HEAD_DIM]
            vt_out[0, hh, :, rows] = v.T.astype(BF16)

    def epilogue_pieces(n, slot):
        row_starts = range(0, tm, CHUNK)
        if kv_only or n == n_tiles - 1:
            return [functools.partial(kv_piece, slot, r0) for r0 in row_starts]
        if n < SGU_GROUPS:
            return [functools.partial(sgu_piece, n, slot, r0) for r0 in row_starts]
        return [functools.partial(q_piece, n - SGU_GROUPS, slot, r0) for r0 in row_starts]

    pending = []
    for n in range(n_tiles):
        slot = n % 2
        for mm, ep in itertools.zip_longest(matmul_pieces(n, slot), pending):
            if mm is not None:
                mm()
            if ep is not None:
                ep()
        pending = epilogue_pieces(n, slot)
    for ep in pending:
        ep()


MOD_SHIFT, MOD_SCALE, MOD_GATE = 0, 1, 2


def _mod_spec(mod_rows, chunk, d, tpb):
    layer, row = mod_rows
    return pl.BlockSpec((1, 1, 1, d), lambda i: (layer, (i // tpb) if row is None else row, 0, chunk))


def _input_projection(x2d, seq, norm_w, mod, mod_rows, w_in_p, k_norm_w, sgu_params, q_norm_w,
                      rope_tables, *, tm, kv_only):
    m, d = x2d.shape
    tpb = seq // tm
    n_tiles = w_in_p.shape[1] // COL_TILE
    rope = rope_tables is not None
    g = SGU_GROUPS
    resident = pl.Buffered(1)

    const2 = lambda i: (0, 0)
    const3 = lambda i: (0, 0, 0)
    if kv_only:
        w_spec = pl.BlockSpec((d, COL_TILE), lambda i: (0, n_tiles - 1), pipeline_mode=resident)
    else:
        w_spec = pl.BlockSpec((d, n_tiles * COL_TILE), const2, pipeline_mode=resident)
    in_specs = [
        pl.BlockSpec((tm, d), lambda i: (i, 0)),
        pl.BlockSpec((1, d), const2),
        _mod_spec(mod_rows, MOD_SCALE, d, tpb),
        _mod_spec(mod_rows, MOD_SHIFT, d, tpb),
        w_spec,
        pl.BlockSpec((1, HEAD_DIM), const2),
    ]
    args = [x2d, norm_w.reshape(1, d), mod, mod, w_in_p, k_norm_w.reshape(1, HEAD_DIM)]
    if not kv_only:
        w_sgu, b_sgu, v_norm_w = sgu_params
        in_specs += [
            pl.BlockSpec((g, CHUNK, CHUNK), const3),
            pl.BlockSpec((g, CHUNK, 1), const3),
            pl.BlockSpec((g, 1, LANES), const3),
            pl.BlockSpec((1, HEAD_DIM), const2),
        ]
        args += [w_sgu.astype(BF16), b_sgu.reshape(g, CHUNK, 1), v_norm_w.reshape(g, 1, LANES),
                 q_norm_w.reshape(1, HEAD_DIM)]
    if rope:
        in_specs += [pl.BlockSpec((tm, HEAD_DIM), lambda i: (i % tpb, 0))] * 2
        args += list(rope_tables)

    out_shape, out_specs = [], []
    if not kv_only:
        out_shape += [jax.ShapeDtypeStruct((m, g * LANES), BF16)] * 2
        out_specs += [pl.BlockSpec((tm, g * LANES), lambda i: (i, 0))] * 2
        q_width = (n_tiles - g - 1) * COL_TILE
        out_shape.append(jax.ShapeDtypeStruct((m, q_width), BF16))
        out_specs.append(pl.BlockSpec((tm, q_width), lambda i: (i, 0)))
    out_shape += [jax.ShapeDtypeStruct((m, N_KV_HEADS * HEAD_DIM), BF16),
                  jax.ShapeDtypeStruct((m // seq, N_KV_HEADS, HEAD_DIM, seq), BF16)]
    out_specs += [pl.BlockSpec((tm, N_KV_HEADS * HEAD_DIM), lambda i: (i, 0)),
                  pl.BlockSpec((1, N_KV_HEADS, HEAD_DIM, tm), lambda i: (i // tpb, 0, 0, i % tpb))]

    return pl.pallas_call(
        functools.partial(_inproj_kernel, tm=tm, rope=rope, kv_only=kv_only),
        out_shape=out_shape,
        grid=(m // tm,),
        in_specs=in_specs,
        out_specs=out_specs,
        scratch_shapes=[pltpu.VMEM((tm, d), BF16), pltpu.VMEM((2, tm, COL_TILE), F32)],
        compiler_params=_params("arbitrary"),
        name="input_projection_kv" if kv_only else "input_projection",
    )(*args)


def _attn_kernel(*refs, n_seg, hp):
    q_ref = refs[0]
    k_refs = refs[1:1 + n_seg]
    vt_refs = refs[1 + n_seg:1 + 2 * n_seg]
    zb_ref, o_ref = refs[1 + 2 * n_seg:]
    nt = (((1,), (1,)), ((), ()))
    tq = q_ref.shape[0]
    heads = [slice(hh * HEAD_DIM, (hh + 1) * HEAD_DIM) for hh in range(hp)]
    q = jnp.concatenate([q_ref[:, sl] for sl in heads], axis=0)
    s = [lax.dot_general(k[...], q, nt, preferred_element_type=F32) for k in k_refs]
    mx = functools.reduce(jnp.maximum, [jnp.max(t, axis=0, keepdims=True) for t in s])
    p = [jnp.exp2(t - mx) for t in s]
    den = functools.reduce(jnp.add, [jnp.sum(t, axis=0, keepdims=True) for t in p])
    o_t = functools.reduce(jnp.add, [
        jnp.dot(vt[0, 0], t.astype(BF16), preferred_element_type=F32) for vt, t in zip(vt_refs, p)])
    o_t = o_t * (1.0 / den)
    for hh, sl in enumerate(heads):
        o = o_t[:, hh * tq:(hh + 1) * tq].T
        o_ref[:, sl] = (o * zb_ref[:, sl].astype(F32)).astype(BF16)


def _attention(q, zb, segments, *, seq_q, tq, hp):
    m, width = q.shape
    n_heads = width // HEAD_DIM
    rep = n_heads // N_KV_HEADS
    nqt = seq_q // tq
    batch = m // seq_q
    assert rep % hp == 0
    q_spec = pl.BlockSpec((tq, hp * HEAD_DIM), lambda b, h, i: (b * nqt + i, h))
    k_specs, vt_specs, ks, vts = [], [], [], []
    for k, vt in segments:
        n = vt.shape[-1]
        k_specs.append(pl.BlockSpec((n, HEAD_DIM), lambda b, h, i: (b, (h * hp) // rep)))
        vt_specs.append(pl.BlockSpec((1, 1, HEAD_DIM, n), lambda b, h, i: (b, (h * hp) // rep, 0, 0)))
        ks.append(k)
        vts.append(vt)
    return pl.pallas_call(
        functools.partial(_attn_kernel, n_seg=len(segments), hp=hp),
        out_shape=jax.ShapeDtypeStruct((m, width), BF16),
        grid=(batch, n_heads // hp, nqt),
        in_specs=[q_spec] + k_specs + vt_specs + [q_spec],
        out_specs=q_spec,
        compiler_params=_params("arbitrary", "arbitrary", "arbitrary"),
        name="attention",
    )(q, *ks, *vts, zb)


def _attn_pipelined_kernel(q_ref, kc_ref, kl_ref, vtc_ref, vtl_ref, zb_ref, o_ref,
                           s_even, mx_even, s_odd, mx_odd, p_scr, o_scr, *, hp):
    t = pl.program_id(0)
    nt = (((1,), (1,)), ((), ()))
    tq = q_ref.shape[0]
    n_ctx = kc_ref.shape[0]
    heads = [slice(hh * HEAD_DIM, (hh + 1) * HEAD_DIM) for hh in range(hp)]

    @pl.when(t == 0)
    def _():
        s_odd[...] = jnp.zeros_like(s_odd)
        mx_odd[...] = jnp.zeros_like(mx_odd)

    def step(s_w, mx_w, s_r, mx_r):
        q = jnp.concatenate([q_ref[:, sl] for sl in heads], axis=0)
        mx = mx_r[...]
        mx8 = den8 = None
        chunks = [(0, kc_ref[...], vtc_ref[0, 0])] + [
            (n_ctx + r0, kl_ref[r0:r0 + QK_ROWS, :], vtl_ref[0, 0, :, r0:r0 + QK_ROWS])
            for r0 in range(0, kl_ref.shape[0], QK_ROWS)]
        def score_chunk(ci):
            nonlocal mx8
            r0, k, _ = chunks[ci]
            sc = lax.dot_general(k, q, nt, preferred_element_type=F32)
            s_w[r0:r0 + k.shape[0], :] = sc
            m8 = _fold_rows(sc, jnp.maximum)
            mx8 = m8 if mx8 is None else jnp.maximum(mx8, m8)

        last = len(chunks) - 1
        for ci, (r0, k, vt) in enumerate(chunks):
            if ci < last:
                score_chunk(ci)
            rows = k.shape[0]
            for r1 in range(r0, r0 + rows, SM_ROWS):
                pc = jnp.exp2(s_r[r1:r1 + SM_ROWS, :] - mx)
                d8 = _fold_rows(pc, jnp.add)
                den8 = d8 if den8 is None else den8 + d8
                p_scr[r1:r1 + SM_ROWS, :] = pc.astype(BF16)
            pv = jnp.dot(vt, p_scr[r0:r0 + rows, :], preferred_element_type=F32)
            if ci == 0:
                o_scr[...] = pv
            else:
                o_scr[...] += pv
        den = jnp.sum(den8, axis=0, keepdims=True)
        o_t = o_scr[...] * (1.0 / den)
        for hh, sl in enumerate(heads):
            o = o_t[:, hh * tq:(hh + 1) * tq].T
            o_ref[:, sl] = (o * zb_ref[:, sl].astype(F32)).astype(BF16)
        score_chunk(last)
        mx_w[...] = jnp.max(mx8, axis=0, keepdims=True)

    pl.when(t % 2 == 0)(lambda: step(s_even, mx_even, s_odd, mx_odd))
    pl.when(t % 2 == 1)(lambda: step(s_odd, mx_odd, s_even, mx_even))


def _attention_pipelined(q, zb, ctx_seg, lat_seg, *, seq_q, tq, hp):
    m, width = q.shape
    n_heads = width // HEAD_DIM
    rep = n_heads // N_KV_HEADS
    assert rep % hp == 0
    n_hg = n_heads // hp
    nqt = seq_q // tq
    n_tiles = (m // seq_q) * n_hg * nqt
    (kc, vtc), (kl, vtl) = ctx_seg, lat_seg
    n_ctx, n_lat = vtc.shape[-1], vtl.shape[-1]

    def decode(tile):
        return tile // (n_hg * nqt), (tile // nqt) % n_hg, tile % nqt

    def cur(t):
        return decode(jnp.minimum(t, n_tiles - 1))

    def prev(t):
        return decode(jnp.maximum(t - 1, 0))

    def q_idx(b, hg, qi):
        return (b * nqt + qi, hg)

    q_block = (tq, hp * HEAD_DIM)
    n_cols = hp * tq
    return pl.pallas_call(
        functools.partial(_attn_pipelined_kernel, hp=hp),
        out_shape=jax.ShapeDtypeStruct((m, width), BF16),
        grid=(n_tiles + 1,),
        in_specs=[
            pl.BlockSpec(q_block, lambda t: q_idx(*cur(t))),
            pl.BlockSpec((n_ctx, HEAD_DIM), lambda t: (cur(t)[0], (cur(t)[1] * hp) // rep)),
            pl.BlockSpec((n_lat, HEAD_DIM), lambda t: (cur(t)[0], (cur(t)[1] * hp) // rep)),
            pl.BlockSpec((1, 1, HEAD_DIM, n_ctx), lambda t: (prev(t)[0], (prev(t)[1] * hp) // rep, 0, 0)),
            pl.BlockSpec((1, 1, HEAD_DIM, n_lat), lambda t: (prev(t)[0], (prev(t)[1] * hp) // rep, 0, 0)),
            pl.BlockSpec(q_block, lambda t: q_idx(*prev(t))),
        ],
        out_specs=pl.BlockSpec(q_block, lambda t: q_idx(*prev(t))),
        scratch_shapes=[pltpu.VMEM((n_ctx + n_lat, n_cols), F32), pltpu.VMEM((1, n_cols), F32)] * 2
        + [pltpu.VMEM((n_ctx + n_lat, n_cols), BF16), pltpu.VMEM((HEAD_DIM, n_cols), F32)],
        compiler_params=_params("arbitrary"),
        name="attention_pipelined",
    )(q, kc, kl, vtc, vtl, zb)


def _outproj_kernel(a_ref, b_ref, w_ref, x_ref, g_ref, o_ref, acc_scr, *, tm):
    ka = a_ref.shape[1]
    d_in, d_out = w_ref.shape

    def matmul_pieces(n, slot):
        cols = slice(n * COL_TILE, (n + 1) * COL_TILE)
        acc = []

        def piece(k0):
            src, off = (a_ref, k0) if k0 < ka else (b_ref, k0 - ka)
            part = jnp.dot(src[:, off:off + K_CHUNK], w_ref[k0:k0 + K_CHUNK, cols], preferred_element_type=F32)
            acc[:] = [part if not acc else acc[0] + part]
            if k0 + K_CHUNK == d_in:
                acc_scr[slot] = acc[0]
        return [functools.partial(piece, k0) for k0 in range(0, d_in, K_CHUNK)]

    def residual_piece(n, slot, r0):
        rows, cols = slice(r0, r0 + CHUNK), slice(n * COL_TILE, (n + 1) * COL_TILE)
        o_ref[rows, cols] = x_ref[rows, cols] + g_ref[0, 0, :, cols] * acc_scr[slot, rows, :]

    pending = []
    for n in range(d_out // COL_TILE):
        slot = n % 2
        for mm, ep in itertools.zip_longest(matmul_pieces(n, slot), pending):
            if mm is not None:
                mm()
            if ep is not None:
                ep()
        pending = [functools.partial(residual_piece, n, slot, r0) for r0 in range(0, tm, CHUNK)]
    for ep in pending:
        ep()


def _output_projection(a, b, w_out, x2d, mod, mod_rows, seq, *, tm):
    m, d = x2d.shape
    tpb = seq // tm
    return pl.pallas_call(
        functools.partial(_outproj_kernel, tm=tm),
        out_shape=jax.ShapeDtypeStruct((m, d), F32),
        grid=(m // tm,),
        in_specs=[
            pl.BlockSpec((tm, a.shape[1]), lambda i: (i, 0)),
            pl.BlockSpec((tm, b.shape[1]), lambda i: (i, 0)),
            pl.BlockSpec(w_out.shape, lambda i: (0, 0), pipeline_mode=pl.Buffered(1)),
            pl.BlockSpec((tm, d), lambda i: (i, 0)),
            _mod_spec(mod_rows, MOD_GATE, d, tpb),
        ],
        out_specs=pl.BlockSpec((tm, d), lambda i: (i, 0)),
        scratch_shapes=[pltpu.VMEM((2, tm, COL_TILE), F32)],
        compiler_params=_params("arbitrary"),
        name="output_projection",
    )(a, b, w_out, x2d, mod)


def _pair_dims(t):
    lead = t.shape[:-1]
    quarter = HEAD_DIM // 4
    t = t.reshape(*lead, -1, 2, 2, quarter)
    return jnp.swapaxes(t, -3, -2).reshape(*lead, -1)


def _rope_tables(n_tokens):
    rows = n_tokens // GRID_W
    row_id = jnp.broadcast_to(jnp.arange(rows)[:, None], (rows, GRID_W)).reshape(-1)
    col_id = jnp.broadcast_to(jnp.arange(GRID_W)[None, :], (rows, GRID_W)).reshape(-1)
    axis_dim = HEAD_DIM // 2
    inv_freq = ROPE_THETA ** (-jnp.arange(0, axis_dim, 2, dtype=F32) / axis_dim)
    ang_r = row_id.astype(F32)[:, None] * inv_freq[None, :]
    ang_c = col_id.astype(F32)[:, None] * inv_freq[None, :]
    ang = _pair_dims(jnp.concatenate([ang_r, ang_r, ang_c, ang_c], axis=-1))
    sign = jnp.where(jnp.arange(HEAD_DIM) < HEAD_DIM // 2, -1.0, 1.0)
    return jnp.cos(ang), jnp.sin(ang) * sign


def _permute_w_in(w, sgu_width, attn_width):
    kv_width = N_KV_HEADS * HEAD_DIM
    q0 = 3 * sgu_width
    k0 = q0 + attn_width
    zb0 = k0 + 2 * kv_width
    d = w.shape[0]
    w = w.astype(BF16)
    branches = jnp.concatenate([w[:, :q0].reshape(d, 3, SGU_GROUPS, LANES),
                                w[:, zb0:].reshape(d, 1, SGU_GROUPS, LANES)], axis=1)
    grouped = jnp.swapaxes(branches, 1, 2).reshape(d, SGU_GROUPS * COL_TILE)
    return jnp.concatenate([grouped, _pair_dims(w[:, q0:k0]), _pair_dims(w[:, k0:k0 + kv_width]),
                            w[:, k0 + kv_width:zb0]], axis=1)


def kernel(x, c, ctx, c_ctx, norm_w, w_mod, b_mod, w_in, w_sgu, b_sgu, v_norm_w, q_norm_w, k_norm_w, w_out):
    batch, seq, d = x.shape
    ctx_len = ctx.shape[1]
    depth = norm_w.shape[0]
    sgu_width = SGU_GROUPS * v_norm_w.shape[-1]
    attn_width = w_out.shape[1] - sgu_width
    assert sgu_width == SGU_GROUPS * LANES and attn_width == sgu_width
    assert w_in.shape[2] == 3 * sgu_width + 2 * attn_width + 2 * N_KV_HEADS * HEAD_DIM

    rows = 8 * pl.cdiv(batch + 1, 8)
    cond = jnp.concatenate([c, c_ctx[None], jnp.zeros((rows - batch - 1, d), F32)], axis=0)
    mod = _modulation(cond, w_mod, b_mod).reshape(depth, rows, 1, 3 * d)

    tables = _rope_tables(seq)
    xl = x.reshape(batch * seq, d)
    xc = ctx.reshape(batch * ctx_len, d)
    tm_lat, tm_ctx = 512, ctx_len

    for layer in range(depth):
        last = layer == depth - 1
        lat_rows, ctx_rows = (layer, None), (layer, batch)
        w_in_p = _permute_w_in(w_in[layer], sgu_width, attn_width)
        w_out_b = w_out[layer].astype(BF16)
        sgu_params = (w_sgu[layer], b_sgu[layer], v_norm_w[layer])
        qnw, knw = _pair_dims(q_norm_w[layer]), _pair_dims(k_norm_w[layer])

        ctx_out = _input_projection(
            xc, ctx_len, norm_w[layer], mod, ctx_rows, w_in_p, knw,
            sgu_params, qnw, None, tm=tm_ctx, kv_only=last)
        kc, vtc = ctx_out[-2:]
        if not last:
            sgu_c, zb_c, q_c = ctx_out[:3]
            attn_c = _attention(q_c, zb_c, [(kc, vtc)], seq_q=ctx_len, tq=ctx_len, hp=ATTN_HEADS_PER_STEP)
            xc_next = _output_projection(sgu_c, attn_c, w_out_b, xc, mod, ctx_rows, ctx_len, tm=tm_ctx)

        sgu_l, zb_l, q_l, kl, vtl = _input_projection(
            xl, seq, norm_w[layer], mod, lat_rows, w_in_p, knw,
            sgu_params, qnw, tables, tm=tm_lat, kv_only=False)
        attn_l = _attention_pipelined(q_l, zb_l, (kc, vtc), (kl, vtl), seq_q=seq, tq=256,
                                      hp=ATTN_HEADS_PER_STEP)
        xl = _output_projection(sgu_l, attn_l, w_out_b, xl, mod, lat_rows, seq, tm=tm_lat)
        if not last:
            xc = xc_next
    return xl.reshape(batch, seq, d)
```

```python
import functools
import itertools
import math

import jax
import jax.numpy as jnp
from jax import lax
from jax.experimental import pallas as pl
from jax.experimental.pallas import tpu as pltpu

F32 = jnp.float32
BF16 = jnp.bfloat16

GRID_W = 64
CHUNK = 128
HEAD_DIM = 128
N_KV_HEADS = 2
SGU_GROUPS = 8
ROPE_THETA = 10000.0
EPS = 1e-6

LANES = 128
SUBLANES = 8
COL_TILE = 4 * LANES
K_CHUNK = 512
NORM_ROWS = 32
VMEM_LIMIT = 56 * 1024 * 1024
ATTN_HEADS_PER_STEP = 2
QK_ROWS = 512
SM_ROWS = 64
Q_SCALE = HEAD_DIM ** -0.5 * math.log2(math.e)


def _params(*sem):
    return pltpu.CompilerParams(dimension_semantics=sem, vmem_limit_bytes=VMEM_LIMIT)


def _rms(t, w):
    return t * lax.rsqrt(jnp.mean(t * t, axis=-1, keepdims=True) + EPS) * w


def _silu(t):
    return t * jax.nn.sigmoid(t)


def _rope(t, cos, sin_signed):
    return t * cos + pltpu.roll(t, HEAD_DIM // 2, 1) * sin_signed


def _fold_rows(x, op):
    parts = [x[i:i + SUBLANES] for i in range(0, x.shape[0], SUBLANES)]
    while len(parts) > 1:
        parts = [op(a, b) for a, b in zip(parts[0::2], parts[1::2])] + ([parts[-1]] if len(parts) % 2 else [])
    return parts[0]


def _mod_kernel(cond_ref, w_ref, b_ref, o_ref):
    a = _silu(cond_ref[...]).astype(BF16)
    o_ref[0] = jnp.dot(a, w_ref[0].astype(BF16), preferred_element_type=F32) + b_ref[0]


def _modulation(cond, w_mod, b_mod, tn=1024):
    depth, d, n = w_mod.shape
    rows = cond.shape[0]
    return pl.pallas_call(
        _mod_kernel,
        out_shape=jax.ShapeDtypeStruct((depth, rows, n), F32),
        grid=(depth, n // tn),
        in_specs=[
            pl.BlockSpec((rows, d), lambda l, j: (0, 0)),
            pl.BlockSpec((1, d, tn), lambda l, j: (l, 0, j)),
            pl.BlockSpec((1, 1, tn), lambda l, j: (l, 0, j)),
        ],
        out_specs=pl.BlockSpec((1, rows, tn), lambda l, j: (l, 0, j)),
        compiler_params=_params("arbitrary", "arbitrary"),
        name="modulation",
    )(cond, w_mod, b_mod.reshape(depth, 1, n))


def _inproj_kernel(*refs, tm, rope, kv_only):
    it = iter(refs)
    x_ref, nw_ref, scale_ref, shift_ref, w_ref, knw_ref = (next(it) for _ in range(6))
    if not kv_only:
        wsgu_ref, bsgu_ref, vnw_ref, qnw_ref = (next(it) for _ in range(4))
    if rope:
        cos_ref, sin_ref = (next(it) for _ in range(2))
    if not kv_only:
        sgu_out, zb_out, q_out = (next(it) for _ in range(3))
    k_out, vt_out, h_scr, acc_scr = (next(it) for _ in range(4))

    d = h_scr.shape[1]
    n_tiles = w_ref.shape[1] // COL_TILE
    heads_per_tile = COL_TILE // HEAD_DIM

    nw, gain, shift = nw_ref[...], 1.0 + scale_ref[0, 0], shift_ref[0, 0]
    for r0 in range(0, tm, NORM_ROWS):
        rows = slice(r0, r0 + NORM_ROWS)
        h_scr[rows, :] = (_rms(x_ref[rows, :], nw) * gain + shift).astype(BF16)

    def maybe_rope(t, rows):
        return _rope(t, cos_ref[rows, :], sin_ref[rows, :]) if rope else t

    def matmul_pieces(n, slot):
        cols = slice(n * COL_TILE, (n + 1) * COL_TILE)
        acc = []

        def piece(k0):
            part = jnp.dot(h_scr[:, k0:k0 + K_CHUNK], w_ref[k0:k0 + K_CHUNK, cols], preferred_element_type=F32)
            acc[:] = [part if not acc else acc[0] + part]
            if k0 + K_CHUNK == d:
                acc_scr[slot] = acc[0]
        return [functools.partial(piece, k0) for k0 in range(0, d, K_CHUNK)]

    def sgu_piece(g, slot, r0):
        rows = slice(r0, r0 + CHUNK)
        a = acc_scr[slot, rows, :]
        gu = jax.nn.gelu(a[:, 0:LANES])
        vn = _rms(jax.nn.gelu(a[:, LANES:2 * LANES]), vnw_ref[g]).astype(BF16)
        s = jnp.dot(wsgu_ref[g], vn, preferred_element_type=F32) + bsgu_ref[g]
        cols = slice(g * LANES, (g + 1) * LANES)
        sgu_out[rows, cols] = (gu * s * _silu(a[:, 2 * LANES:3 * LANES])).astype(BF16)
        zb_out[rows, cols] = _silu(a[:, 3 * LANES:4 * LANES]).astype(BF16)

    def q_piece(qt, slot, r0):
        rows = slice(r0, r0 + CHUNK)
        a = acc_scr[slot, rows, :]
        for hh in range(heads_per_tile):
            qn = maybe_rope(_rms(a[:, hh * HEAD_DIM:(hh + 1) * HEAD_DIM], qnw_ref[...]), rows)
            c0 = (qt * heads_per_tile + hh) * HEAD_DIM
            q_out[rows, c0:c0 + HEAD_DIM] = (qn * Q_SCALE).astype(BF16)

    def kv_piece(slot, r0):
        rows = slice(r0, r0 + CHUNK)
        a = acc_scr[slot, rows, :]
        for hh in range(N_KV_HEADS):
            cols = slice(hh * HEAD_DIM, (hh + 1) * HEAD_DIM)
            k_out[rows, cols] = maybe_rope(_rms(a[:, cols], knw_ref[...]), rows).astype(BF16)
            v = a[:, (N_KV_HEADS + hh) * HEAD_DIM:(N_KV_HEADS + hh + 1) * HEAD_DIM]
            vt_out[0, hh, :, rows] = v.T.astype(BF16)

    def epilogue_pieces(n, slot):
        row_starts = range(0, tm, CHUNK)
        if kv_only or n == n_tiles - 1:
            return [functools.partial(kv_piece, slot, r0) for r0 in row_starts]
        if n < SGU_GROUPS:
            return [functools.partial(sgu_piece, n, slot, r0) for r0 in row_starts]
        return [functools.partial(q_piece, n - SGU_GROUPS, slot, r0) for r0 in row_starts]

    pending = []
    for n in range(n_tiles):
        slot = n % 2
        for mm, ep in itertools.zip_longest(matmul_pieces(n, slot), pending):
            if mm is not None:
                mm()
            if ep is not None:
                ep()
        pending = epilogue_pieces(n, slot)
    for ep in pending:
        ep()


MOD_SHIFT, MOD_SCALE, MOD_GATE = 0, 1, 2


def _mod_spec(mod_rows, chunk, d, tpb):
    layer, row = mod_rows
    return pl.BlockSpec((1, 1, 1, d), lambda i: (layer, (i // tpb) if row is None else row, 0, chunk))


def _input_projection(x2d, seq, norm_w, mod, mod_rows, w_in_p, k_norm_w, sgu_params, q_norm_w,
                      rope_tables, *, tm, kv_only):
    m, d = x2d.shape
    tpb = seq // tm
    n_tiles = w_in_p.shape[2] // COL_TILE
    rope = rope_tables is not None
    g = SGU_GROUPS
    resident = pl.Buffered(1)
    layer = mod_rows[0]

    const2 = lambda i: (0, 0)
    const3 = lambda i: (0, 0, 0)
    if kv_only:
        w_spec = pl.BlockSpec((None, d, COL_TILE), lambda i: (layer, 0, n_tiles - 1), pipeline_mode=resident)
    else:
        w_spec = pl.BlockSpec((None, d, n_tiles * COL_TILE), lambda i: (layer, 0, 0), pipeline_mode=resident)
    in_specs = [
        pl.BlockSpec((tm, d), lambda i: (i, 0)),
        pl.BlockSpec((1, d), const2),
        _mod_spec(mod_rows, MOD_SCALE, d, tpb),
        _mod_spec(mod_rows, MOD_SHIFT, d, tpb),
        w_spec,
        pl.BlockSpec((1, HEAD_DIM), const2),
    ]
    args = [x2d, norm_w.reshape(1, d), mod, mod, w_in_p, k_norm_w.reshape(1, HEAD_DIM)]
    if not kv_only:
        w_sgu, b_sgu, v_norm_w = sgu_params
        in_specs += [
            pl.BlockSpec((g, CHUNK, CHUNK), const3),
            pl.BlockSpec((g, CHUNK, 1), const3),
            pl.BlockSpec((g, 1, LANES), const3),
            pl.BlockSpec((1, HEAD_DIM), const2),
        ]
        args += [w_sgu.astype(BF16), b_sgu.reshape(g, CHUNK, 1), v_norm_w.reshape(g, 1, LANES),
                 q_norm_w.reshape(1, HEAD_DIM)]
    if rope:
        in_specs += [pl.BlockSpec((tm, HEAD_DIM), lambda i: (i % tpb, 0))] * 2
        args += list(rope_tables)

    out_shape, out_specs = [], []
    if not kv_only:
        out_shape += [jax.ShapeDtypeStruct((m, g * LANES), BF16)] * 2
        out_specs += [pl.BlockSpec((tm, g * LANES), lambda i: (i, 0))] * 2
        q_width = (n_tiles - g - 1) * COL_TILE
        out_shape.append(jax.ShapeDtypeStruct((m, q_width), BF16))
        out_specs.append(pl.BlockSpec((tm, q_width), lambda i: (i, 0)))
    out_shape += [jax.ShapeDtypeStruct((m, N_KV_HEADS * HEAD_DIM), BF16),
                  jax.ShapeDtypeStruct((m // seq, N_KV_HEADS, HEAD_DIM, seq), BF16)]
    out_specs += [pl.BlockSpec((tm, N_KV_HEADS * HEAD_DIM), lambda i: (i, 0)),
                  pl.BlockSpec((1, N_KV_HEADS, HEAD_DIM, tm), lambda i: (i // tpb, 0, 0, i % tpb))]

    return pl.pallas_call(
        functools.partial(_inproj_kernel, tm=tm, rope=rope, kv_only=kv_only),
        out_shape=out_shape,
        grid=(m // tm,),
        in_specs=in_specs,
        out_specs=out_specs,
        scratch_shapes=[pltpu.VMEM((tm, d), BF16), pltpu.VMEM((2, tm, COL_TILE), F32)],
        compiler_params=_params("arbitrary"),
        name="input_projection_kv" if kv_only else "input_projection",
    )(*args)


def _attn_kernel(*refs, n_seg, hp):
    q_ref = refs[0]
    k_refs = refs[1:1 + n_seg]
    vt_refs = refs[1 + n_seg:1 + 2 * n_seg]
    zb_ref, o_ref = refs[1 + 2 * n_seg:]
    nt = (((1,), (1,)), ((), ()))
    tq = q_ref.shape[0]
    heads = [slice(hh * HEAD_DIM, (hh + 1) * HEAD_DIM) for hh in range(hp)]
    q = jnp.concatenate([q_ref[:, sl] for sl in heads], axis=0)
    s = [lax.dot_general(k[...], q, nt, preferred_element_type=F32) for k in k_refs]
    mx = functools.reduce(jnp.maximum, [jnp.max(t, axis=0, keepdims=True) for t in s])
    p = [jnp.exp2(t - mx) for t in s]
    den = functools.reduce(jnp.add, [jnp.sum(t, axis=0, keepdims=True) for t in p])
    o_t = functools.reduce(jnp.add, [
        jnp.dot(vt[0, 0], t.astype(BF16), preferred_element_type=F32) for vt, t in zip(vt_refs, p)])
    o_t = o_t * (1.0 / den)
    for hh, sl in enumerate(heads):
        o = o_t[:, hh * tq:(hh + 1) * tq].T
        o_ref[:, sl] = (o * zb_ref[:, sl].astype(F32)).astype(BF16)


def _attention(q, zb, segments, *, seq_q, tq, hp):
    m, width = q.shape
    n_heads = width // HEAD_DIM
    rep = n_heads // N_KV_HEADS
    nqt = seq_q // tq
    batch = m // seq_q
    assert rep % hp == 0
    q_spec = pl.BlockSpec((tq, hp * HEAD_DIM), lambda b, h, i: (b * nqt + i, h))
    k_specs, vt_specs, ks, vts = [], [], [], []
    for k, vt in segments:
        n = vt.shape[-1]
        k_specs.append(pl.BlockSpec((n, HEAD_DIM), lambda b, h, i: (b, (h * hp) // rep)))
        vt_specs.append(pl.BlockSpec((1, 1, HEAD_DIM, n), lambda b, h, i: (b, (h * hp) // rep, 0, 0)))
        ks.append(k)
        vts.append(vt)
    return pl.pallas_call(
        functools.partial(_attn_kernel, n_seg=len(segments), hp=hp),
        out_shape=jax.ShapeDtypeStruct((m, width), BF16),
        grid=(batch, n_heads // hp, nqt),
        in_specs=[q_spec] + k_specs + vt_specs + [q_spec],
        out_specs=q_spec,
        compiler_params=_params("arbitrary", "arbitrary", "arbitrary"),
        name="attention",
    )(q, *ks, *vts, zb)


def _attn_pipelined_kernel(q_ref, kc_ref, kl_ref, vtc_ref, vtl_ref, zb_ref, o_ref,
                           s_even, mx_even, s_odd, mx_odd, p_scr, o_scr, *, hp):
    t = pl.program_id(0)
    nt = (((1,), (1,)), ((), ()))
    tq = q_ref.shape[0]
    n_ctx = kc_ref.shape[0]
    heads = [slice(hh * HEAD_DIM, (hh + 1) * HEAD_DIM) for hh in range(hp)]

    @pl.when(t == 0)
    def _():
        s_odd[...] = jnp.zeros_like(s_odd)
        mx_odd[...] = jnp.zeros_like(mx_odd)

    def step(s_w, mx_w, s_r, mx_r):
        q = jnp.concatenate([q_ref[:, sl] for sl in heads], axis=0)
        mx = mx_r[...]
        mx8 = den8 = None
        chunks = [(0, kc_ref[...], vtc_ref[0, 0])] + [
            (n_ctx + r0, kl_ref[r0:r0 + QK_ROWS, :], vtl_ref[0, 0, :, r0:r0 + QK_ROWS])
            for r0 in range(0, kl_ref.shape[0], QK_ROWS)]
        def score_chunk(ci):
            nonlocal mx8
            r0, k, _ = chunks[ci]
            sc = lax.dot_general(k, q, nt, preferred_element_type=F32)
            s_w[r0:r0 + k.shape[0], :] = sc
            m8 = _fold_rows(sc, jnp.maximum)
            mx8 = m8 if mx8 is None else jnp.maximum(mx8, m8)

        last = len(chunks) - 1
        for ci, (r0, k, vt) in enumerate(chunks):
            if ci < last:
                score_chunk(ci)
            rows = k.shape[0]
            for r1 in range(r0, r0 + rows, SM_ROWS):
                pc = jnp.exp2(s_r[r1:r1 + SM_ROWS, :] - mx)
                d8 = _fold_rows(pc, jnp.add)
                den8 = d8 if den8 is None else den8 + d8
                p_scr[r1:r1 + SM_ROWS, :] = pc.astype(BF16)
            pv = jnp.dot(vt, p_scr[r0:r0 + rows, :], preferred_element_type=F32)
            if ci == 0:
                o_scr[...] = pv
            else:
                o_scr[...] += pv
        den = jnp.sum(den8, axis=0, keepdims=True)
        o_t = o_scr[...] * (1.0 / den)
        for hh, sl in enumerate(heads):
            o = o_t[:, hh * tq:(hh + 1) * tq].T
            o_ref[:, sl] = (o * zb_ref[:, sl].astype(F32)).astype(BF16)
        score_chunk(last)
        mx_w[...] = jnp.max(mx8, axis=0, keepdims=True)

    pl.when(t % 2 == 0)(lambda: step(s_even, mx_even, s_odd, mx_odd))
    pl.when(t % 2 == 1)(lambda: step(s_odd, mx_odd, s_even, mx_even))


def _attention_pipelined(q, zb, ctx_seg, lat_seg, *, seq_q, tq, hp):
    m, width = q.shape
    n_heads = width // HEAD_DIM
    rep = n_heads // N_KV_HEADS
    assert rep % hp == 0
    n_hg = n_heads // hp
    nqt = seq_q // tq
    n_tiles = (m // seq_q) * n_hg * nqt
    (kc, vtc), (kl, vtl) = ctx_seg, lat_seg
    n_ctx, n_lat = vtc.shape[-1], vtl.shape[-1]

    def decode(tile):
        return tile // (n_hg * nqt), (tile // nqt) % n_hg, tile % nqt

    def cur(t):
        return decode(jnp.minimum(t, n_tiles - 1))

    def prev(t):
        return decode(jnp.maximum(t - 1, 0))

    def q_idx(b, hg, qi):
        return (b * nqt + qi, hg)

    q_block = (tq, hp * HEAD_DIM)
    n_cols = hp * tq
    return pl.pallas_call(
        functools.partial(_attn_pipelined_kernel, hp=hp),
        out_shape=jax.ShapeDtypeStruct((m, width), BF16),
        grid=(n_tiles + 1,),
        in_specs=[
            pl.BlockSpec(q_block, lambda t: q_idx(*cur(t))),
            pl.BlockSpec((n_ctx, HEAD_DIM), lambda t: (cur(t)[0], (cur(t)[1] * hp) // rep)),
            pl.BlockSpec((n_lat, HEAD_DIM), lambda t: (cur(t)[0], (cur(t)[1] * hp) // rep)),
            pl.BlockSpec((1, 1, HEAD_DIM, n_ctx), lambda t: (prev(t)[0], (prev(t)[1] * hp) // rep, 0, 0)),
            pl.BlockSpec((1, 1, HEAD_DIM, n_lat), lambda t: (prev(t)[0], (prev(t)[1] * hp) // rep, 0, 0)),
            pl.BlockSpec(q_block, lambda t: q_idx(*prev(t))),
        ],
        out_specs=pl.BlockSpec(q_block, lambda t: q_idx(*prev(t))),
        scratch_shapes=[pltpu.VMEM((n_ctx + n_lat, n_cols), F32), pltpu.VMEM((1, n_cols), F32)] * 2
        + [pltpu.VMEM((n_ctx + n_lat, n_cols), BF16), pltpu.VMEM((HEAD_DIM, n_cols), F32)],
        compiler_params=_params("arbitrary"),
        name="attention_pipelined",
    )(q, kc, kl, vtc, vtl, zb)


def _outproj_kernel(a_ref, b_ref, w_ref, x_ref, g_ref, o_ref, acc_scr, *, tm):
    ka = a_ref.shape[1]
    d_in, d_out = w_ref.shape

    def matmul_pieces(n, slot):
        cols = slice(n * COL_TILE, (n + 1) * COL_TILE)
        acc = []

        def piece(k0):
            src, off = (a_ref, k0) if k0 < ka else (b_ref, k0 - ka)
            part = jnp.dot(src[:, off:off + K_CHUNK], w_ref[k0:k0 + K_CHUNK, cols], preferred_element_type=F32)
            acc[:] = [part if not acc else acc[0] + part]
            if k0 + K_CHUNK == d_in:
                acc_scr[slot] = acc[0]
        return [functools.partial(piece, k0) for k0 in range(0, d_in, K_CHUNK)]

    def residual_piece(n, slot, r0):
        rows, cols = slice(r0, r0 + CHUNK), slice(n * COL_TILE, (n + 1) * COL_TILE)
        o_ref[rows, cols] = x_ref[rows, cols] + g_ref[0, 0, :, cols] * acc_scr[slot, rows, :]

    pending = []
    for n in range(d_out // COL_TILE):
        slot = n % 2
        for mm, ep in itertools.zip_longest(matmul_pieces(n, slot), pending):
            if mm is not None:
                mm()
            if ep is not None:
                ep()
        pending = [functools.partial(residual_piece, n, slot, r0) for r0 in range(0, tm, CHUNK)]
    for ep in pending:
        ep()


def _output_projection(a, b, w_out, x2d, mod, mod_rows, seq, *, tm):
    m, d = x2d.shape
    tpb = seq // tm
    return pl.pallas_call(
        functools.partial(_outproj_kernel, tm=tm),
        out_shape=jax.ShapeDtypeStruct((m, d), F32),
        grid=(m // tm,),
        in_specs=[
            pl.BlockSpec((tm, a.shape[1]), lambda i: (i, 0)),
            pl.BlockSpec((tm, b.shape[1]), lambda i: (i, 0)),
            pl.BlockSpec((None,) + w_out.shape[1:], lambda i: (mod_rows[0], 0, 0), pipeline_mode=pl.Buffered(1)),
            pl.BlockSpec((tm, d), lambda i: (i, 0)),
            _mod_spec(mod_rows, MOD_GATE, d, tpb),
        ],
        out_specs=pl.BlockSpec((tm, d), lambda i: (i, 0)),
        scratch_shapes=[pltpu.VMEM((2, tm, COL_TILE), F32)],
        compiler_params=_params("arbitrary"),
        name="output_projection",
    )(a, b, w_out, x2d, mod)


def _pair_dims(t):
    lead = t.shape[:-1]
    quarter = HEAD_DIM // 4
    t = t.reshape(*lead, -1, 2, 2, quarter)
    return jnp.swapaxes(t, -3, -2).reshape(*lead, -1)


def _rope_tables(n_tokens):
    rows = n_tokens // GRID_W
    row_id = jnp.broadcast_to(jnp.arange(rows)[:, None], (rows, GRID_W)).reshape(-1)
    col_id = jnp.broadcast_to(jnp.arange(GRID_W)[None, :], (rows, GRID_W)).reshape(-1)
    axis_dim = HEAD_DIM // 2
    inv_freq = ROPE_THETA ** (-jnp.arange(0, axis_dim, 2, dtype=F32) / axis_dim)
    ang_r = row_id.astype(F32)[:, None] * inv_freq[None, :]
    ang_c = col_id.astype(F32)[:, None] * inv_freq[None, :]
    ang = _pair_dims(jnp.concatenate([ang_r, ang_r, ang_c, ang_c], axis=-1))
    sign = jnp.where(jnp.arange(HEAD_DIM) < HEAD_DIM // 2, -1.0, 1.0)
    return jnp.cos(ang), jnp.sin(ang) * sign


def _pair_lanes(t):
    quarter = HEAD_DIM // 4
    lane = lax.broadcasted_iota(jnp.int32, t.shape, 1)
    from_right = pltpu.roll(t, HEAD_DIM - quarter, 1)
    from_left = pltpu.roll(t, quarter, 1)
    second = (lane >= quarter) & (lane < 2 * quarter)
    third = (lane >= 2 * quarter) & (lane < 3 * quarter)
    return jnp.where(second, from_right, jnp.where(third, from_left, t))


def _w_in_layout_kernel(u_ref, v_ref, za_ref, zb_ref, qkv_ref, o_ref):
    n = pl.program_id(1)

    @pl.when(n < SGU_GROUPS)
    def _():
        for i, ref in enumerate((u_ref, v_ref, za_ref, zb_ref)):
            o_ref[0, :, i * LANES:(i + 1) * LANES] = ref[0].astype(BF16)

    @pl.when(n >= SGU_GROUPS)
    def _():
        is_kv_tile = n == pl.num_programs(1) - 1
        for hh in range(COL_TILE // HEAD_DIM):
            cols = slice(hh * HEAD_DIM, (hh + 1) * HEAD_DIM)
            t = qkv_ref[0, :, cols]
            paired = _pair_lanes(t)
            if hh >= N_KV_HEADS:
                paired = jnp.where(is_kv_tile, t, paired)
            o_ref[0, :, cols] = paired.astype(BF16)


def _layout_w_in(w_in, sgu_width, attn_width):
    depth, d, d_in = w_in.shape
    g = SGU_GROUPS
    kv_width = N_KV_HEADS * HEAD_DIM
    q0 = 3 * sgu_width
    zb0 = q0 + attn_width + 2 * kv_width
    n_tiles = d_in // COL_TILE
    assert q0 % COL_TILE == 0 and zb0 % LANES == 0 and 2 * kv_width == COL_TILE

    def branch_spec(base):
        return pl.BlockSpec((1, d, LANES), lambda l, n: (l, 0, base // LANES + jnp.minimum(n, g - 1)))

    return pl.pallas_call(
        _w_in_layout_kernel,
        out_shape=jax.ShapeDtypeStruct((depth, d, d_in), BF16),
        grid=(depth, n_tiles),
        in_specs=[branch_spec(0), branch_spec(sgu_width), branch_spec(2 * sgu_width), branch_spec(zb0),
                  pl.BlockSpec((1, d, COL_TILE), lambda l, n: (l, 0, q0 // COL_TILE + jnp.maximum(n - g, 0)))],
        out_specs=pl.BlockSpec((1, d, COL_TILE), lambda l, n: (l, 0, n)),
        compiler_params=_params("arbitrary", "arbitrary"),
        name="w_in_layout",
    )(w_in, w_in, w_in, w_in, w_in)


def _cast_kernel(x_ref, o_ref):
    o_ref[...] = x_ref[...].astype(o_ref.dtype)


def _cast_bf16(w, rows=512):
    depth, r, c = w.shape
    spec = pl.BlockSpec((1, rows, c), lambda l, i: (l, i, 0))
    return pl.pallas_call(
        _cast_kernel,
        out_shape=jax.ShapeDtypeStruct(w.shape, BF16),
        grid=(depth, r // rows),
        in_specs=[spec],
        out_specs=spec,
        compiler_params=_params("arbitrary", "arbitrary"),
        name="cast_bf16",
    )(w)


def kernel(x, c, ctx, c_ctx, norm_w, w_mod, b_mod, w_in, w_sgu, b_sgu, v_norm_w, q_norm_w, k_norm_w, w_out):
    batch, seq, d = x.shape
    ctx_len = ctx.shape[1]
    depth = norm_w.shape[0]
    sgu_width = SGU_GROUPS * v_norm_w.shape[-1]
    attn_width = w_out.shape[1] - sgu_width
    assert sgu_width == SGU_GROUPS * LANES and attn_width == sgu_width
    assert w_in.shape[2] == 3 * sgu_width + 2 * attn_width + 2 * N_KV_HEADS * HEAD_DIM

    rows = 8 * pl.cdiv(batch + 1, 8)
    cond = jnp.concatenate([c, c_ctx[None], jnp.zeros((rows - batch - 1, d), F32)], axis=0)
    mod = _modulation(cond, w_mod, b_mod).reshape(depth, rows, 1, 3 * d)

    tables = _rope_tables(seq)
    xl = x.reshape(batch * seq, d)
    xc = ctx.reshape(batch * ctx_len, d)
    tm_lat, tm_ctx = 512, ctx_len
    w_in_p = _layout_w_in(w_in, sgu_width, attn_width)
    w_out_b = _cast_bf16(w_out)

    for layer in range(depth):
        last = layer == depth - 1
        lat_rows, ctx_rows = (layer, None), (layer, batch)
        sgu_params = (w_sgu[layer], b_sgu[layer], v_norm_w[layer])
        qnw, knw = _pair_dims(q_norm_w[layer]), _pair_dims(k_norm_w[layer])

        ctx_out = _input_projection(
            xc, ctx_len, norm_w[layer], mod, ctx_rows, w_in_p, knw,
            sgu_params, qnw, None, tm=tm_ctx, kv_only=last)
        kc, vtc = ctx_out[-2:]
        if not last:
            sgu_c, zb_c, q_c = ctx_out[:3]
            attn_c = _attention(q_c, zb_c, [(kc, vtc)], seq_q=ctx_len, tq=ctx_len, hp=ATTN_HEADS_PER_STEP)
            xc_next = _output_projection(sgu_c, attn_c, w_out_b, xc, mod, ctx_rows, ctx_len, tm=tm_ctx)

        sgu_l, zb_l, q_l, kl, vtl = _input_projection(
            xl, seq, norm_w[layer], mod, lat_rows, w_in_p, knw,
            sgu_params, qnw, tables, tm=tm_lat, kv_only=False)
        attn_l = _attention_pipelined(q_l, zb_l, (kc, vtc), (kl, vtl), seq_q=seq, tq=256,
                                      hp=ATTN_HEADS_PER_STEP)
        xl = _output_projection(sgu_l, attn_l, w_out_b, xl, mod, lat_rows, seq, tm=tm_lat)
        if not last:
            xc = xc_next
    return xl.reshape(batch, seq, d)
```

```python
import functools
import itertools
import math

import jax
import jax.numpy as jnp
from jax import lax
from jax.experimental import pallas as pl
from jax.experimental.pallas import tpu as pltpu

F32 = jnp.float32
BF16 = jnp.bfloat16

GRID_W = 64
CHUNK = 128
HEAD_DIM = 128
N_KV_HEADS = 2
SGU_GROUPS = 8
ROPE_THETA = 10000.0
EPS = 1e-6

LANES = 128
SUBLANES = 8
COL_TILE = 4 * LANES
K_CHUNK = 512
NORM_ROWS = 32
VMEM_LIMIT = 56 * 1024 * 1024
ATTN_HEADS_PER_STEP = 2
QK_ROWS = 512
SM_ROWS = 64
Q_SCALE = HEAD_DIM ** -0.5 * math.log2(math.e)


def _params(*sem):
    return pltpu.CompilerParams(dimension_semantics=sem, vmem_limit_bytes=VMEM_LIMIT)


def _rms(t, w):
    return t * lax.rsqrt(jnp.mean(t * t, axis=-1, keepdims=True) + EPS) * w


def _silu(t):
    return t * jax.nn.sigmoid(t)


def _rope(t, cos, sin_signed):
    return t * cos + pltpu.roll(t, HEAD_DIM // 2, 1) * sin_signed


def _fold_rows(x, op):
    parts = [x[i:i + SUBLANES] for i in range(0, x.shape[0], SUBLANES)]
    while len(parts) > 1:
        parts = [op(a, b) for a, b in zip(parts[0::2], parts[1::2])] + ([parts[-1]] if len(parts) % 2 else [])
    return parts[0]


def _mod_kernel(cond_ref, w_ref, b_ref, o_ref):
    a = _silu(cond_ref[...]).astype(BF16)
    o_ref[0] = jnp.dot(a, w_ref[0].astype(BF16), preferred_element_type=F32) + b_ref[0]


def _modulation(cond, w_mod, b_mod, tn=1024):
    depth, d, n = w_mod.shape
    rows = cond.shape[0]
    return pl.pallas_call(
        _mod_kernel,
        out_shape=jax.ShapeDtypeStruct((depth, rows, n), F32),
        grid=(depth, n // tn),
        in_specs=[
            pl.BlockSpec((rows, d), lambda l, j: (0, 0)),
            pl.BlockSpec((1, d, tn), lambda l, j: (l, 0, j)),
            pl.BlockSpec((1, 1, tn), lambda l, j: (l, 0, j)),
        ],
        out_specs=pl.BlockSpec((1, rows, tn), lambda l, j: (l, 0, j)),
        compiler_params=_params("arbitrary", "arbitrary"),
        name="modulation",
    )(cond, w_mod, b_mod.reshape(depth, 1, n))


def _inproj_kernel(*refs, tm, rope, kv_only):
    it = iter(refs)
    x_ref, nw_ref, scale_ref, shift_ref, w_ref, knw_ref = (next(it) for _ in range(6))
    if not kv_only:
        wsgu_ref, bsgu_ref, vnw_ref, qnw_ref = (next(it) for _ in range(4))
    if rope:
        cos_ref, sin_ref = (next(it) for _ in range(2))
    if not kv_only:
        sgu_out, zb_out, q_out = (next(it) for _ in range(3))
    k_out, vt_out, h_scr, acc_scr = (next(it) for _ in range(4))

    d = h_scr.shape[1]
    n_tiles = w_ref.shape[1] // COL_TILE

    nw, gain, shift = nw_ref[...], 1.0 + scale_ref[0, 0], shift_ref[0, 0]
    for r0 in range(0, tm, NORM_ROWS):
        rows = slice(r0, r0 + NORM_ROWS)
        h_scr[rows, :] = (_rms(x_ref[rows, :], nw) * gain + shift).astype(BF16)

    def maybe_rope(t, rows):
        return _rope(t, cos_ref[rows, :], sin_ref[rows, :]) if rope else t

    def matmul_pieces(n, slot):
        cols = slice(n * COL_TILE, (n + 1) * COL_TILE)
        acc = []

        def piece(k0):
            part = jnp.dot(h_scr[:, k0:k0 + K_CHUNK], w_ref[k0:k0 + K_CHUNK, cols], preferred_element_type=F32)
            acc[:] = [part if not acc else acc[0] + part]
            if k0 + K_CHUNK == d:
                acc_scr[slot] = acc[0]
        return [functools.partial(piece, k0) for k0 in range(0, d, K_CHUNK)]

    def acc_block(slot, rows, i):
        return acc_scr[slot, rows, i * LANES:(i + 1) * LANES]

    def sgu_piece(g, slot, r0):
        rows = slice(r0, r0 + CHUNK)
        vn = _rms(jax.nn.gelu(acc_block(slot, rows, 1)), vnw_ref[g]).astype(BF16)
        s = jnp.dot(wsgu_ref[g], vn, preferred_element_type=F32) + bsgu_ref[g]
        cols = slice(g * LANES, (g + 1) * LANES)
        gated = jax.nn.gelu(acc_block(slot, rows, 0)) * _silu(acc_block(slot, rows, 2))
        sgu_out[rows, cols] = (gated * s).astype(BF16)
        zb_out[rows, cols] = _silu(acc_block(slot, rows, 3)).astype(BF16)

    def q_piece(qt, slot, r0):
        rows = slice(r0, r0 + CHUNK)
        for hh in range(COL_TILE // HEAD_DIM):
            qn = maybe_rope(_rms(acc_block(slot, rows, hh), qnw_ref[...]), rows)
            c0 = qt * COL_TILE + hh * HEAD_DIM
            q_out[rows, c0:c0 + HEAD_DIM] = (qn * Q_SCALE).astype(BF16)

    def kv_piece(slot, r0):
        rows = slice(r0, r0 + CHUNK)
        for hh in range(N_KV_HEADS):
            k = maybe_rope(_rms(acc_block(slot, rows, hh), knw_ref[...]), rows)
            k_out[rows, hh * HEAD_DIM:(hh + 1) * HEAD_DIM] = k.astype(BF16)
            vt_out[0, hh, :, rows] = acc_block(slot, rows, N_KV_HEADS + hh).T.astype(BF16)

    def epilogue_pieces(n, slot):
        row_starts = range(0, tm, CHUNK)
        if kv_only or n == n_tiles - 1:
            return [functools.partial(kv_piece, slot, r0) for r0 in row_starts]
        if n < SGU_GROUPS:
            return [functools.partial(sgu_piece, n, slot, r0) for r0 in row_starts]
        return [functools.partial(q_piece, n - SGU_GROUPS, slot, r0) for r0 in row_starts]

    def row_matmul_pieces(n, slot):
        cols = slice(n * COL_TILE, (n + 1) * COL_TILE)

        def piece(r0):
            rows = slice(r0, r0 + CHUNK)
            acc_scr[slot, rows, :] = jnp.dot(h_scr[rows, :], w_ref[:, cols], preferred_element_type=F32)
        return [functools.partial(piece, r0) for r0 in range(0, tm, CHUNK)]

    pending = []
    for n in range(n_tiles - 1):
        slot = n % 2
        for mm, ep in itertools.zip_longest(matmul_pieces(n, slot), pending):
            if mm is not None:
                mm()
            if ep is not None:
                ep()
        pending = epilogue_pieces(n, slot)
    n = n_tiles - 1
    slot = n % 2
    own = []
    for mm, ep, prev in itertools.zip_longest(row_matmul_pieces(n, slot), epilogue_pieces(n, slot), pending):
        if prev is not None:
            prev()
        mm()
        for piece in own:
            piece()
        own = [ep]
    for piece in own:
        piece()


MOD_SHIFT, MOD_SCALE, MOD_GATE = 0, 1, 2


def _mod_spec(mod_rows, chunk, d, tpb):
    layer, row = mod_rows
    return pl.BlockSpec((1, 1, 1, d), lambda i: (layer, (i // tpb) if row is None else row, 0, chunk))


def _input_projection(x2d, seq, norm_w, mod, mod_rows, w_in_p, k_norm_w, sgu_params, q_norm_w,
                      rope_tables, *, tm, kv_only):
    m, d = x2d.shape
    tpb = seq // tm
    n_tiles = w_in_p.shape[2] // COL_TILE
    rope = rope_tables is not None
    g = SGU_GROUPS
    resident = pl.Buffered(1)
    layer = mod_rows[0]

    const2 = lambda i: (0, 0)
    const3 = lambda i: (0, 0, 0)
    if kv_only:
        w_spec = pl.BlockSpec((None, d, COL_TILE), lambda i: (layer, 0, n_tiles - 1), pipeline_mode=resident)
    else:
        w_spec = pl.BlockSpec((None, d, n_tiles * COL_TILE), lambda i: (layer, 0, 0), pipeline_mode=resident)
    in_specs = [
        pl.BlockSpec((tm, d), lambda i: (i, 0)),
        pl.BlockSpec((1, d), const2),
        _mod_spec(mod_rows, MOD_SCALE, d, tpb),
        _mod_spec(mod_rows, MOD_SHIFT, d, tpb),
        w_spec,
        pl.BlockSpec((1, HEAD_DIM), const2),
    ]
    args = [x2d, norm_w.reshape(1, d), mod, mod, w_in_p, k_norm_w.reshape(1, HEAD_DIM)]
    if not kv_only:
        w_sgu, b_sgu, v_norm_w = sgu_params
        in_specs += [
            pl.BlockSpec((g, CHUNK, CHUNK), const3),
            pl.BlockSpec((g, CHUNK, 1), const3),
            pl.BlockSpec((g, 1, LANES), const3),
            pl.BlockSpec((1, HEAD_DIM), const2),
        ]
        args += [w_sgu.astype(BF16), b_sgu.reshape(g, CHUNK, 1), v_norm_w.reshape(g, 1, LANES),
                 q_norm_w.reshape(1, HEAD_DIM)]
    if rope:
        in_specs += [pl.BlockSpec((tm, HEAD_DIM), lambda i: (i % tpb, 0))] * 2
        args += list(rope_tables)

    out_shape, out_specs = [], []
    if not kv_only:
        for width in (g * LANES, g * LANES, (n_tiles - g - 1) * COL_TILE):
            out_shape.append(jax.ShapeDtypeStruct((m, width), BF16))
            out_specs.append(pl.BlockSpec((tm, width), lambda i: (i, 0)))
    out_shape += [jax.ShapeDtypeStruct((m, N_KV_HEADS * HEAD_DIM), BF16),
                  jax.ShapeDtypeStruct((m // seq, N_KV_HEADS, HEAD_DIM, seq), BF16)]
    out_specs += [pl.BlockSpec((tm, N_KV_HEADS * HEAD_DIM), lambda i: (i, 0)),
                  pl.BlockSpec((1, N_KV_HEADS, HEAD_DIM, tm), lambda i: (i // tpb, 0, 0, i % tpb))]

    return pl.pallas_call(
        functools.partial(_inproj_kernel, tm=tm, rope=rope, kv_only=kv_only),
        out_shape=out_shape,
        grid=(m // tm,),
        in_specs=in_specs,
        out_specs=out_specs,
        scratch_shapes=[pltpu.VMEM((tm, d), BF16), pltpu.VMEM((2, tm, COL_TILE), F32)],
        compiler_params=_params("arbitrary"),
        name="input_projection_kv" if kv_only else "input_projection",
    )(*args)


def _attn_kernel(*refs, n_seg, hp):
    q_ref = refs[0]
    k_refs = refs[1:1 + n_seg]
    vt_refs = refs[1 + n_seg:1 + 2 * n_seg]
    zb_ref, o_ref = refs[1 + 2 * n_seg:]
    nt = (((1,), (1,)), ((), ()))
    tq = q_ref.shape[0]
    heads = [slice(hh * HEAD_DIM, (hh + 1) * HEAD_DIM) for hh in range(hp)]
    q = jnp.concatenate([q_ref[:, sl] for sl in heads], axis=0)
    s = [lax.dot_general(k[...], q, nt, preferred_element_type=F32) for k in k_refs]
    mx = functools.reduce(jnp.maximum, [jnp.max(t, axis=0, keepdims=True) for t in s])
    p = [jnp.exp2(t - mx) for t in s]
    den = functools.reduce(jnp.add, [jnp.sum(t, axis=0, keepdims=True) for t in p])
    o_t = functools.reduce(jnp.add, [
        jnp.dot(vt[0, 0], t.astype(BF16), preferred_element_type=F32) for vt, t in zip(vt_refs, p)])
    o_t = o_t * (1.0 / den)
    for hh, sl in enumerate(heads):
        o = o_t[:, hh * tq:(hh + 1) * tq].T
        o_ref[:, sl] = (o * zb_ref[:, sl].astype(F32)).astype(BF16)


def _attention(q, zb, segments, *, seq_q, tq, hp):
    m, width = q.shape
    n_heads = width // HEAD_DIM
    rep = n_heads // N_KV_HEADS
    nqt = seq_q // tq
    batch = m // seq_q
    assert rep % hp == 0
    q_spec = pl.BlockSpec((tq, hp * HEAD_DIM), lambda b, h, i: (b * nqt + i, h))
    k_specs, vt_specs, ks, vts = [], [], [], []
    for k, vt in segments:
        n = vt.shape[-1]
        k_specs.append(pl.BlockSpec((n, HEAD_DIM), lambda b, h, i: (b, (h * hp) // rep)))
        vt_specs.append(pl.BlockSpec((1, 1, HEAD_DIM, n), lambda b, h, i: (b, (h * hp) // rep, 0, 0)))
        ks.append(k)
        vts.append(vt)
    return pl.pallas_call(
        functools.partial(_attn_kernel, n_seg=len(segments), hp=hp),
        out_shape=jax.ShapeDtypeStruct((m, width), BF16),
        grid=(batch, n_heads // hp, nqt),
        in_specs=[q_spec] + k_specs + vt_specs + [q_spec],
        out_specs=q_spec,
        compiler_params=_params("arbitrary", "arbitrary", "arbitrary"),
        name="attention",
    )(q, *ks, *vts, zb)


def _attn_pipelined_kernel(q_ref, kc_ref, kl_ref, vtc_ref, vtl_ref, zb_ref, o_ref,
                           s_even, mx_even, s_odd, mx_odd, p_scr, o_scr, *, hp):
    t = pl.program_id(0)
    nt = (((1,), (1,)), ((), ()))
    tq = q_ref.shape[0]
    n_ctx = kc_ref.shape[0]
    heads = [slice(hh * HEAD_DIM, (hh + 1) * HEAD_DIM) for hh in range(hp)]

    @pl.when(t == 0)
    def _():
        s_odd[...] = jnp.zeros_like(s_odd)
        mx_odd[...] = jnp.zeros_like(mx_odd)

    def step(s_w, mx_w, s_r, mx_r):
        q = jnp.concatenate([q_ref[:, sl] for sl in heads], axis=0)
        mx = mx_r[...]
        mx8 = den8 = None
        chunks = [(0, kc_ref[...], vtc_ref[0, 0])] + [
            (n_ctx + r0, kl_ref[r0:r0 + QK_ROWS, :], vtl_ref[0, 0, :, r0:r0 + QK_ROWS])
            for r0 in range(0, kl_ref.shape[0], QK_ROWS)]

        def score_chunk(ci):
            nonlocal mx8
            r0, k, _ = chunks[ci]
            sc = lax.dot_general(k, q, nt, preferred_element_type=F32)
            s_w[r0:r0 + k.shape[0], :] = sc
            m8 = _fold_rows(sc, jnp.maximum)
            mx8 = m8 if mx8 is None else jnp.maximum(mx8, m8)

        last = len(chunks) - 1
        for ci, (r0, k, vt) in enumerate(chunks):
            if ci < last:
                score_chunk(ci)
            rows = k.shape[0]
            for r1 in range(r0, r0 + rows, SM_ROWS):
                pc = jnp.exp2(s_r[r1:r1 + SM_ROWS, :] - mx)
                d8 = _fold_rows(pc, jnp.add)
                den8 = d8 if den8 is None else den8 + d8
                p_scr[r1:r1 + SM_ROWS, :] = pc.astype(BF16)
            pv = jnp.dot(vt, p_scr[r0:r0 + rows, :], preferred_element_type=F32)
            if ci == 0:
                o_scr[...] = pv
            else:
                o_scr[...] += pv
        den = jnp.sum(den8, axis=0, keepdims=True)
        o_t = o_scr[...] * (1.0 / den)
        for hh, sl in enumerate(heads):
            o = o_t[:, hh * tq:(hh + 1) * tq].T
            o_ref[:, sl] = (o * zb_ref[:, sl].astype(F32)).astype(BF16)
        score_chunk(last)
        mx_w[...] = jnp.max(mx8, axis=0, keepdims=True)

    pl.when(t % 2 == 0)(lambda: step(s_even, mx_even, s_odd, mx_odd))
    pl.when(t % 2 == 1)(lambda: step(s_odd, mx_odd, s_even, mx_even))


def _attention_pipelined(q, zb, ctx_seg, lat_seg, *, seq_q, tq, hp):
    m, width = q.shape
    n_heads = width // HEAD_DIM
    rep = n_heads // N_KV_HEADS
    assert rep % hp == 0
    n_hg = n_heads // hp
    nqt = seq_q // tq
    n_tiles = (m // seq_q) * n_hg * nqt
    (kc, vtc), (kl, vtl) = ctx_seg, lat_seg
    n_ctx, n_lat = vtc.shape[-1], vtl.shape[-1]

    def decode(tile):
        return tile // (n_hg * nqt), (tile // nqt) % n_hg, tile % nqt

    def cur(t):
        return decode(jnp.minimum(t, n_tiles - 1))

    def prev(t):
        return decode(jnp.maximum(t - 1, 0))

    def q_idx(b, hg, qi):
        return (b * nqt + qi, hg)

    q_block = (tq, hp * HEAD_DIM)
    n_cols = hp * tq
    return pl.pallas_call(
        functools.partial(_attn_pipelined_kernel, hp=hp),
        out_shape=jax.ShapeDtypeStruct((m, width), BF16),
        grid=(n_tiles + 1,),
        in_specs=[
            pl.BlockSpec(q_block, lambda t: q_idx(*cur(t))),
            pl.BlockSpec((n_ctx, HEAD_DIM), lambda t: (cur(t)[0], (cur(t)[1] * hp) // rep)),
            pl.BlockSpec((n_lat, HEAD_DIM), lambda t: (cur(t)[0], (cur(t)[1] * hp) // rep)),
            pl.BlockSpec((1, 1, HEAD_DIM, n_ctx), lambda t: (prev(t)[0], (prev(t)[1] * hp) // rep, 0, 0)),
            pl.BlockSpec((1, 1, HEAD_DIM, n_lat), lambda t: (prev(t)[0], (prev(t)[1] * hp) // rep, 0, 0)),
            pl.BlockSpec(q_block, lambda t: q_idx(*prev(t))),
        ],
        out_specs=pl.BlockSpec(q_block, lambda t: q_idx(*prev(t))),
        scratch_shapes=[pltpu.VMEM((n_ctx + n_lat, n_cols), F32), pltpu.VMEM((1, n_cols), F32)] * 2
        + [pltpu.VMEM((n_ctx + n_lat, n_cols), BF16), pltpu.VMEM((HEAD_DIM, n_cols), F32)],
        compiler_params=_params("arbitrary"),
        name="attention_pipelined",
    )(q, kc, kl, vtc, vtl, zb)


def _outproj_kernel(a_ref, b_ref, w_ref, x_ref, g_ref, o_ref, acc_scr, *, tm):
    ka = a_ref.shape[1]
    d_in, d_out = w_ref.shape

    def matmul_pieces(n, slot):
        cols = slice(n * COL_TILE, (n + 1) * COL_TILE)
        acc = []

        def piece(k0):
            src, off = (a_ref, k0) if k0 < ka else (b_ref, k0 - ka)
            part = jnp.dot(src[:, off:off + K_CHUNK], w_ref[k0:k0 + K_CHUNK, cols], preferred_element_type=F32)
            acc[:] = [part if not acc else acc[0] + part]
            if k0 + K_CHUNK == d_in:
                acc_scr[slot] = acc[0]
        return [functools.partial(piece, k0) for k0 in range(0, d_in, K_CHUNK)]

    def residual_piece(n, slot, r0):
        rows, cols = slice(r0, r0 + CHUNK), slice(n * COL_TILE, (n + 1) * COL_TILE)
        o_ref[rows, cols] = x_ref[rows, cols] + g_ref[0, 0, :, cols] * acc_scr[slot, rows, :]

    pending = []
    for n in range(d_out // COL_TILE):
        slot = n % 2
        for mm, ep in itertools.zip_longest(matmul_pieces(n, slot), pending):
            if mm is not None:
                mm()
            if ep is not None:
                ep()
        pending = [functools.partial(residual_piece, n, slot, r0) for r0 in range(0, tm, CHUNK)]
    for ep in pending:
        ep()


def _output_projection(a, b, w_out, x2d, mod, mod_rows, seq, *, tm):
    m, d = x2d.shape
    tpb = seq // tm
    return pl.pallas_call(
        functools.partial(_outproj_kernel, tm=tm),
        out_shape=jax.ShapeDtypeStruct((m, d), F32),
        grid=(m // tm,),
        in_specs=[
            pl.BlockSpec((tm, a.shape[1]), lambda i: (i, 0)),
            pl.BlockSpec((tm, b.shape[1]), lambda i: (i, 0)),
            pl.BlockSpec((None,) + w_out.shape[1:], lambda i: (mod_rows[0], 0, 0), pipeline_mode=pl.Buffered(1)),
            pl.BlockSpec((tm, d), lambda i: (i, 0)),
            _mod_spec(mod_rows, MOD_GATE, d, tpb),
        ],
        out_specs=pl.BlockSpec((tm, d), lambda i: (i, 0)),
        scratch_shapes=[pltpu.VMEM((2, tm, COL_TILE), F32)],
        compiler_params=_params("arbitrary"),
        name="output_projection",
    )(a, b, w_out, x2d, mod)


def _pair_dims(t):
    lead = t.shape[:-1]
    quarter = HEAD_DIM // 4
    t = t.reshape(*lead, -1, 2, 2, quarter)
    return jnp.swapaxes(t, -3, -2).reshape(*lead, -1)


def _rope_tables(n_tokens):
    rows = n_tokens // GRID_W
    row_id = jnp.broadcast_to(jnp.arange(rows)[:, None], (rows, GRID_W)).reshape(-1)
    col_id = jnp.broadcast_to(jnp.arange(GRID_W)[None, :], (rows, GRID_W)).reshape(-1)
    axis_dim = HEAD_DIM // 2
    inv_freq = ROPE_THETA ** (-jnp.arange(0, axis_dim, 2, dtype=F32) / axis_dim)
    ang_r = row_id.astype(F32)[:, None] * inv_freq[None, :]
    ang_c = col_id.astype(F32)[:, None] * inv_freq[None, :]
    ang = _pair_dims(jnp.concatenate([ang_r, ang_r, ang_c, ang_c], axis=-1))
    sign = jnp.where(jnp.arange(HEAD_DIM) < HEAD_DIM // 2, -1.0, 1.0)
    return jnp.cos(ang), jnp.sin(ang) * sign


def _pair_lanes(t):
    quarter = HEAD_DIM // 4
    lane = lax.broadcasted_iota(jnp.int32, t.shape, 1)
    from_right = pltpu.roll(t, HEAD_DIM - quarter, 1)
    from_left = pltpu.roll(t, quarter, 1)
    second = (lane >= quarter) & (lane < 2 * quarter)
    third = (lane >= 2 * quarter) & (lane < 3 * quarter)
    return jnp.where(second, from_right, jnp.where(third, from_left, t))


def _w_in_layout_kernel(u_ref, v_ref, za_ref, zb_ref, qkv_ref, o_ref):
    n = pl.program_id(1)

    @pl.when(n < SGU_GROUPS)
    def _():
        for i, ref in enumerate((u_ref, v_ref, za_ref, zb_ref)):
            o_ref[0, :, i * LANES:(i + 1) * LANES] = ref[0].astype(BF16)

    @pl.when(n >= SGU_GROUPS)
    def _():
        is_kv_tile = n == pl.num_programs(1) - 1
        for hh in range(COL_TILE // HEAD_DIM):
            cols = slice(hh * HEAD_DIM, (hh + 1) * HEAD_DIM)
            t = qkv_ref[0, :, cols]
            paired = _pair_lanes(t)
            if hh >= N_KV_HEADS:
                paired = jnp.where(is_kv_tile, t, paired)
            o_ref[0, :, cols] = paired.astype(BF16)


def _layout_w_in(w_in, sgu_width, attn_width):
    depth, d, d_in = w_in.shape
    g = SGU_GROUPS
    kv_width = N_KV_HEADS * HEAD_DIM
    q0 = 3 * sgu_width
    zb0 = q0 + attn_width + 2 * kv_width
    n_tiles = d_in // COL_TILE
    assert q0 % COL_TILE == 0 and zb0 % LANES == 0 and 2 * kv_width == COL_TILE

    def branch_spec(base):
        return pl.BlockSpec((1, d, LANES), lambda l, n: (l, 0, base // LANES + jnp.minimum(n, g - 1)))

    return pl.pallas_call(
        _w_in_layout_kernel,
        out_shape=jax.ShapeDtypeStruct((depth, d, d_in), BF16),
        grid=(depth, n_tiles),
        in_specs=[branch_spec(0), branch_spec(sgu_width), branch_spec(2 * sgu_width), branch_spec(zb0),
                  pl.BlockSpec((1, d, COL_TILE), lambda l, n: (l, 0, q0 // COL_TILE + jnp.maximum(n - g, 0)))],
        out_specs=pl.BlockSpec((1, d, COL_TILE), lambda l, n: (l, 0, n)),
        compiler_params=_params("arbitrary", "arbitrary"),
        name="w_in_layout",
    )(w_in, w_in, w_in, w_in, w_in)


def _cast_kernel(x_ref, o_ref):
    o_ref[...] = x_ref[...].astype(o_ref.dtype)


def _cast_bf16(w, rows=512):
    depth, r, c = w.shape
    spec = pl.BlockSpec((1, rows, c), lambda l, i: (l, i, 0))
    return pl.pallas_call(
        _cast_kernel,
        out_shape=jax.ShapeDtypeStruct(w.shape, BF16),
        grid=(depth, r // rows),
        in_specs=[spec],
        out_specs=spec,
        compiler_params=_params("arbitrary", "arbitrary"),
        name="cast_bf16",
    )(w)


def kernel(x, c, ctx, c_ctx, norm_w, w_mod, b_mod, w_in, w_sgu, b_sgu, v_norm_w, q_norm_w, k_norm_w, w_out):
    batch, seq, d = x.shape
    ctx_len = ctx.shape[1]
    depth = norm_w.shape[0]
    sgu_width = SGU_GROUPS * v_norm_w.shape[-1]
    attn_width = w_out.shape[1] - sgu_width
    assert sgu_width == SGU_GROUPS * LANES and attn_width == sgu_width
    assert w_in.shape[2] == 3 * sgu_width + 2 * attn_width + 2 * N_KV_HEADS * HEAD_DIM

    rows = 8 * pl.cdiv(batch + 1, 8)
    cond = jnp.concatenate([c, c_ctx[None], jnp.zeros((rows - batch - 1, d), F32)], axis=0)
    mod = _modulation(cond, w_mod, b_mod).reshape(depth, rows, 1, 3 * d)

    tables = _rope_tables(seq)
    xl = x.reshape(batch * seq, d)
    xc = ctx.reshape(batch * ctx_len, d)
    tm_lat, tm_ctx = 512, ctx_len
    w_in_p = _layout_w_in(w_in, sgu_width, attn_width)
    w_out_b = _cast_bf16(w_out)

    for layer in range(depth):
        last = layer == depth - 1
        lat_rows, ctx_rows = (layer, None), (layer, batch)
        sgu_params = (w_sgu[layer], b_sgu[layer], v_norm_w[layer])
        qnw, knw = _pair_dims(q_norm_w[layer]), _pair_dims(k_norm_w[layer])

        ctx_out = _input_projection(
            xc, ctx_len, norm_w[layer], mod, ctx_rows, w_in_p, knw,
            sgu_params, qnw, None, tm=tm_ctx, kv_only=last)
        kc, vtc = ctx_out[-2:]
        if not last:
            sgu_c, zb_c, q_c = ctx_out[:3]
            attn_c = _attention(q_c, zb_c, [(kc, vtc)], seq_q=ctx_len, tq=ctx_len, hp=ATTN_HEADS_PER_STEP)
            xc_next = _output_projection(sgu_c, attn_c, w_out_b, xc, mod, ctx_rows, ctx_len, tm=tm_ctx)

        sgu_l, zb_l, q_l, kl, vtl = _input_projection(
            xl, seq, norm_w[layer], mod, lat_rows, w_in_p, knw,
            sgu_params, qnw, tables, tm=tm_lat, kv_only=False)
        attn_l = _attention_pipelined(q_l, zb_l, (kc, vtc), (kl, vtl), seq_q=seq, tq=256,
                                      hp=ATTN_HEADS_PER_STEP)
        xl = _output_projection(sgu_l, attn_l, w_out_b, xl, mod, lat_rows, seq, tm=tm_lat)
        if not last:
            xc = xc_next
    return xl.reshape(batch, seq, d)
```

```python
import functools
import itertools
import math

import jax
import jax.numpy as jnp
from jax import lax
from jax.experimental import pallas as pl
from jax.experimental.pallas import tpu as pltpu

F32 = jnp.float32
BF16 = jnp.bfloat16

GRID_W = 64
CHUNK = 128
HEAD_DIM = 128
N_KV_HEADS = 2
SGU_GROUPS = 8
ROPE_THETA = 10000.0
EPS = 1e-6

LANES = 128
SUBLANES = 8
COL_TILE = 4 * LANES
K_CHUNK = 512
NORM_ROWS = 32
VMEM_LIMIT = 56 * 1024 * 1024
ATTN_HEADS_PER_STEP = 2
QK_ROWS = 512
SM_ROWS = 64
Q_SCALE = HEAD_DIM ** -0.5 * math.log2(math.e)


def _params(*sem):
    return pltpu.CompilerParams(dimension_semantics=sem, vmem_limit_bytes=VMEM_LIMIT)


def _rms(t, w):
    return t * lax.rsqrt(jnp.mean(t * t, axis=-1, keepdims=True) + EPS) * w


def _silu(t):
    return t * jax.nn.sigmoid(t)


def _rope(t, cos, sin_signed):
    return t * cos + pltpu.roll(t, HEAD_DIM // 2, 1) * sin_signed


def _fold_rows(x, op):
    parts = [x[i:i + SUBLANES] for i in range(0, x.shape[0], SUBLANES)]
    while len(parts) > 1:
        parts = [op(a, b) for a, b in zip(parts[0::2], parts[1::2])] + ([parts[-1]] if len(parts) % 2 else [])
    return parts[0]


def _mod_kernel(cond_ref, w_ref, b_ref, o_ref):
    a = _silu(cond_ref[...]).astype(BF16)
    o_ref[0] = jnp.dot(a, w_ref[0].astype(BF16), preferred_element_type=F32) + b_ref[0]


def _modulation(cond, w_mod, b_mod, tn=1024):
    depth, d, n = w_mod.shape
    rows = cond.shape[0]
    return pl.pallas_call(
        _mod_kernel,
        out_shape=jax.ShapeDtypeStruct((depth, rows, n), F32),
        grid=(depth, n // tn),
        in_specs=[
            pl.BlockSpec((rows, d), lambda l, j: (0, 0)),
            pl.BlockSpec((1, d, tn), lambda l, j: (l, 0, j)),
            pl.BlockSpec((1, 1, tn), lambda l, j: (l, 0, j)),
        ],
        out_specs=pl.BlockSpec((1, rows, tn), lambda l, j: (l, 0, j)),
        compiler_params=_params("arbitrary", "arbitrary"),
        name="modulation",
    )(cond, w_mod, b_mod.reshape(depth, 1, n))


def _inproj_kernel(*refs, tm, rope, kv_only):
    it = iter(refs)
    x_ref, nw_ref, scale_ref, shift_ref, w_ref, knw_ref = (next(it) for _ in range(6))
    if not kv_only:
        wsgu_ref, bsgu_ref, vnw_ref = (next(it) for _ in range(3))
    if rope:
        cos_ref, sin_ref = (next(it) for _ in range(2))
    if not kv_only:
        sgu_out, zb_out, q_out = (next(it) for _ in range(3))
    k_out, vt_out, h_scr, acc_scr = (next(it) for _ in range(4))

    d = h_scr.shape[1]
    n_tiles = w_ref.shape[1] // COL_TILE

    nw, gain, shift = nw_ref[...], 1.0 + scale_ref[0, 0], shift_ref[0, 0]
    for r0 in range(0, tm, NORM_ROWS):
        rows = slice(r0, r0 + NORM_ROWS)
        h_scr[rows, :] = (_rms(x_ref[rows, :], nw) * gain + shift).astype(BF16)

    def maybe_rope(t, rows):
        return _rope(t, cos_ref[rows, :], sin_ref[rows, :]) if rope else t

    def matmul_pieces(n, slot):
        cols = slice(n * COL_TILE, (n + 1) * COL_TILE)
        acc = []

        def piece(k0):
            part = jnp.dot(h_scr[:, k0:k0 + K_CHUNK], w_ref[k0:k0 + K_CHUNK, cols], preferred_element_type=F32)
            acc[:] = [part if not acc else acc[0] + part]
            if k0 + K_CHUNK == d:
                acc_scr[slot] = acc[0]
        return [functools.partial(piece, k0) for k0 in range(0, d, K_CHUNK)]

    def acc_block(slot, rows, i):
        return acc_scr[slot, rows, i * LANES:(i + 1) * LANES]

    def sgu_piece(g, slot, r0):
        rows = slice(r0, r0 + CHUNK)
        vn = _rms(jax.nn.gelu(acc_block(slot, rows, 1)), vnw_ref[g]).astype(BF16)
        s = jnp.dot(wsgu_ref[g], vn, preferred_element_type=F32) + bsgu_ref[g]
        cols = slice(g * LANES, (g + 1) * LANES)
        gated = jax.nn.gelu(acc_block(slot, rows, 0)) * _silu(acc_block(slot, rows, 2))
        sgu_out[rows, cols] = (gated * s).astype(BF16)
        zb_out[rows, cols] = _silu(acc_block(slot, rows, 3)).astype(BF16)

    def q_piece(qt, slot, r0):
        rows = slice(r0, r0 + CHUNK)
        for hh in range(COL_TILE // HEAD_DIM):
            c0 = qt * COL_TILE + hh * HEAD_DIM
            q_out[rows, c0:c0 + HEAD_DIM] = acc_block(slot, rows, hh)

    def kv_piece(slot, r0):
        rows = slice(r0, r0 + CHUNK)
        for hh in range(N_KV_HEADS):
            k = maybe_rope(_rms(acc_block(slot, rows, hh), knw_ref[...]), rows)
            k_out[rows, hh * HEAD_DIM:(hh + 1) * HEAD_DIM] = k.astype(BF16)
            vt_out[0, hh, :, rows] = acc_block(slot, rows, N_KV_HEADS + hh).T.astype(BF16)

    def epilogue_pieces(n, slot):
        row_starts = range(0, tm, CHUNK)
        if kv_only or n == n_tiles - 1:
            return [functools.partial(kv_piece, slot, r0) for r0 in row_starts]
        if n < SGU_GROUPS:
            return [functools.partial(sgu_piece, n, slot, r0) for r0 in row_starts]
        return [functools.partial(q_piece, n - SGU_GROUPS, slot, r0) for r0 in row_starts]

    def row_matmul_pieces(n, slot):
        cols = slice(n * COL_TILE, (n + 1) * COL_TILE)

        def piece(r0):
            rows = slice(r0, r0 + CHUNK)
            acc_scr[slot, rows, :] = jnp.dot(h_scr[rows, :], w_ref[:, cols], preferred_element_type=F32)
        return [functools.partial(piece, r0) for r0 in range(0, tm, CHUNK)]

    pending = []
    for n in range(n_tiles - 1):
        slot = n % 2
        for mm, ep in itertools.zip_longest(matmul_pieces(n, slot), pending):
            if mm is not None:
                mm()
            if ep is not None:
                ep()
        pending = epilogue_pieces(n, slot)
    n = n_tiles - 1
    slot = n % 2
    own = []
    for mm, ep, prev in itertools.zip_longest(row_matmul_pieces(n, slot), epilogue_pieces(n, slot), pending):
        if prev is not None:
            prev()
        mm()
        for piece in own:
            piece()
        own = [ep]
    for piece in own:
        piece()


MOD_SHIFT, MOD_SCALE, MOD_GATE = 0, 1, 2


def _mod_spec(mod_rows, chunk, d, tpb):
    layer, row = mod_rows
    return pl.BlockSpec((1, 1, 1, d), lambda i: (layer, (i // tpb) if row is None else row, 0, chunk))


def _input_projection(x2d, seq, norm_w, mod, mod_rows, w_in_p, k_norm_w, sgu_params,
                      rope_tables, *, tm, kv_only):
    m, d = x2d.shape
    tpb = seq // tm
    n_tiles = w_in_p.shape[2] // COL_TILE
    rope = rope_tables is not None
    g = SGU_GROUPS
    resident = pl.Buffered(1)
    layer = mod_rows[0]

    const2 = lambda i: (0, 0)
    const3 = lambda i: (0, 0, 0)
    if kv_only:
        w_spec = pl.BlockSpec((None, d, COL_TILE), lambda i: (layer, 0, n_tiles - 1), pipeline_mode=resident)
    else:
        w_spec = pl.BlockSpec((None, d, n_tiles * COL_TILE), lambda i: (layer, 0, 0), pipeline_mode=resident)
    in_specs = [
        pl.BlockSpec((tm, d), lambda i: (i, 0)),
        pl.BlockSpec((1, d), const2),
        _mod_spec(mod_rows, MOD_SCALE, d, tpb),
        _mod_spec(mod_rows, MOD_SHIFT, d, tpb),
        w_spec,
        pl.BlockSpec((1, HEAD_DIM), const2),
    ]
    args = [x2d, norm_w.reshape(1, d), mod, mod, w_in_p, k_norm_w.reshape(1, HEAD_DIM)]
    if not kv_only:
        w_sgu, b_sgu, v_norm_w = sgu_params
        in_specs += [
            pl.BlockSpec((g, CHUNK, CHUNK), const3),
            pl.BlockSpec((g, CHUNK, 1), const3),
            pl.BlockSpec((g, 1, LANES), const3),
        ]
        args += [w_sgu.astype(BF16), b_sgu.reshape(g, CHUNK, 1), v_norm_w.reshape(g, 1, LANES)]
    if rope:
        in_specs += [pl.BlockSpec((tm, HEAD_DIM), lambda i: (i % tpb, 0))] * 2
        args += list(rope_tables)

    out_shape, out_specs = [], []
    if not kv_only:
        for width, dtype in ((g * LANES, BF16), (g * LANES, BF16), ((n_tiles - g - 1) * COL_TILE, F32)):
            out_shape.append(jax.ShapeDtypeStruct((m, width), dtype))
            out_specs.append(pl.BlockSpec((tm, width), lambda i: (i, 0)))
    out_shape += [jax.ShapeDtypeStruct((m, N_KV_HEADS * HEAD_DIM), BF16),
                  jax.ShapeDtypeStruct((m // seq, N_KV_HEADS, HEAD_DIM, seq), BF16)]
    out_specs += [pl.BlockSpec((tm, N_KV_HEADS * HEAD_DIM), lambda i: (i, 0)),
                  pl.BlockSpec((1, N_KV_HEADS, HEAD_DIM, tm), lambda i: (i // tpb, 0, 0, i % tpb))]

    return pl.pallas_call(
        functools.partial(_inproj_kernel, tm=tm, rope=rope, kv_only=kv_only),
        out_shape=out_shape,
        grid=(m // tm,),
        in_specs=in_specs,
        out_specs=out_specs,
        scratch_shapes=[pltpu.VMEM((tm, d), BF16), pltpu.VMEM((2, tm, COL_TILE), F32)],
        compiler_params=_params("arbitrary"),
        name="input_projection_kv" if kv_only else "input_projection",
    )(*args)


def _prepare_q(q_raw, qnw, rope_tables):
    parts = []
    for c0 in range(0, q_raw.shape[1], HEAD_DIM):
        t = _rms(q_raw[:, c0:c0 + HEAD_DIM], qnw)
        if rope_tables is not None:
            t = _rope(t, *rope_tables)
        parts.append((t * Q_SCALE).astype(BF16))
    return jnp.concatenate(parts, axis=0)


def _attn_kernel(*refs, n_seg, hp):
    q_ref, qnw_ref = refs[:2]
    refs = refs[1:]
    k_refs = refs[1:1 + n_seg]
    vt_refs = refs[1 + n_seg:1 + 2 * n_seg]
    zb_ref, o_ref = refs[1 + 2 * n_seg:]
    nt = (((1,), (1,)), ((), ()))
    tq = q_ref.shape[0]
    heads = [slice(hh * HEAD_DIM, (hh + 1) * HEAD_DIM) for hh in range(hp)]
    q = _prepare_q(q_ref[...], qnw_ref[...], None)
    s = [lax.dot_general(k[...], q, nt, preferred_element_type=F32) for k in k_refs]
    mx = functools.reduce(jnp.maximum, [jnp.max(t, axis=0, keepdims=True) for t in s])
    p = [jnp.exp2(t - mx) for t in s]
    den = functools.reduce(jnp.add, [jnp.sum(t, axis=0, keepdims=True) for t in p])
    o_t = functools.reduce(jnp.add, [
        jnp.dot(vt[0, 0], t.astype(BF16), preferred_element_type=F32) for vt, t in zip(vt_refs, p)])
    o_t = o_t * (1.0 / den)
    for hh, sl in enumerate(heads):
        o = o_t[:, hh * tq:(hh + 1) * tq].T
        o_ref[:, sl] = (o * zb_ref[:, sl].astype(F32)).astype(BF16)


def _attention(q, q_norm_w, zb, segments, *, seq_q, tq, hp):
    m, width = q.shape
    n_heads = width // HEAD_DIM
    rep = n_heads // N_KV_HEADS
    nqt = seq_q // tq
    batch = m // seq_q
    assert rep % hp == 0
    q_spec = pl.BlockSpec((tq, hp * HEAD_DIM), lambda b, h, i: (b * nqt + i, h))
    k_specs, vt_specs, ks, vts = [], [], [], []
    for k, vt in segments:
        n = vt.shape[-1]
        k_specs.append(pl.BlockSpec((n, HEAD_DIM), lambda b, h, i: (b, (h * hp) // rep)))
        vt_specs.append(pl.BlockSpec((1, 1, HEAD_DIM, n), lambda b, h, i: (b, (h * hp) // rep, 0, 0)))
        ks.append(k)
        vts.append(vt)
    return pl.pallas_call(
        functools.partial(_attn_kernel, n_seg=len(segments), hp=hp),
        out_shape=jax.ShapeDtypeStruct((m, width), BF16),
        grid=(batch, n_heads // hp, nqt),
        in_specs=[q_spec, pl.BlockSpec((1, HEAD_DIM), lambda b, h, i: (0, 0))] + k_specs + vt_specs + [q_spec],
        out_specs=q_spec,
        compiler_params=_params("arbitrary", "arbitrary", "arbitrary"),
        name="attention",
    )(q, q_norm_w.reshape(1, HEAD_DIM), *ks, *vts, zb)


def _attn_pipelined_kernel(q_ref, cos_ref, sin_ref, qnw_ref, kc_ref, kl_ref, vtc_ref, vtl_ref, zb_ref, o_ref,
                           qn_even, s_even, mx_even, qn_odd, s_odd, mx_odd, p_scr, o_scr, *, hp):
    t = pl.program_id(0)
    nt = (((1,), (1,)), ((), ()))
    tq = q_ref.shape[0]
    n_ctx = kc_ref.shape[0]
    tiles_per_step = qn_even.shape[0]

    @pl.when(t == 0)
    def _():
        qn_odd[...] = jnp.zeros_like(qn_odd)
        s_even[...] = jnp.zeros_like(s_even)
        mx_even[...] = jnp.zeros_like(mx_even)

    def tile(ti, qn_w, qn_r, s_w, mx_w, s_r, mx_r):
        heads = [slice((ti * hp + hh) * HEAD_DIM, (ti * hp + hh + 1) * HEAD_DIM) for hh in range(hp)]
        tile_cols = slice(ti * hp * HEAD_DIM, (ti + 1) * hp * HEAD_DIM)
        s_w, mx_w, s_r, mx_r = s_w.at[ti], mx_w.at[ti], s_r.at[ti], mx_r.at[ti]
        qn_w[ti] = _prepare_q(q_ref[:, tile_cols], qnw_ref[...], (cos_ref[...], sin_ref[...]))
        q = qn_r[ti]
        mx = mx_r[...]
        mx8 = den8 = None
        chunks = [(0, kc_ref[...], vtc_ref[0, 0])] + [
            (n_ctx + r0, kl_ref[r0:r0 + QK_ROWS, :], vtl_ref[0, 0, :, r0:r0 + QK_ROWS])
            for r0 in range(0, kl_ref.shape[0], QK_ROWS)]

        def score_chunk(ci):
            nonlocal mx8
            r0, k, _ = chunks[ci]
            sc = lax.dot_general(k, q, nt, preferred_element_type=F32)
            s_w[r0:r0 + k.shape[0], :] = sc
            m8 = _fold_rows(sc, jnp.maximum)
            mx8 = m8 if mx8 is None else jnp.maximum(mx8, m8)

        last = len(chunks) - 1
        for ci, (r0, k, vt) in enumerate(chunks):
            if ci < last:
                score_chunk(ci)
            rows = k.shape[0]
            for r1 in range(r0, r0 + rows, SM_ROWS):
                pc = jnp.exp2(s_r[r1:r1 + SM_ROWS, :] - mx)
                d8 = _fold_rows(pc, jnp.add)
                den8 = d8 if den8 is None else den8 + d8
                p_scr[r1:r1 + SM_ROWS, :] = pc.astype(BF16)
            pv = jnp.dot(vt, p_scr[r0:r0 + rows, :], preferred_element_type=F32)
            if ci == 0:
                o_scr[...] = pv
            else:
                o_scr[...] += pv
        den = jnp.sum(den8, axis=0, keepdims=True)
        o_t = o_scr[...] * (1.0 / den)
        for hh, sl in enumerate(heads):
            o = o_t[:, hh * tq:(hh + 1) * tq].T
            o_ref[:, sl] = (o * zb_ref[:, sl].astype(F32)).astype(BF16)
        score_chunk(last)
        mx_w[...] = jnp.max(mx8, axis=0, keepdims=True)

    def step(*buffers):
        for ti in range(tiles_per_step):
            pl.when(t + ti >= 0)(functools.partial(tile, ti, *buffers))

    pl.when(t % 2 == 0)(lambda: step(qn_even, qn_odd, s_odd, mx_odd, s_even, mx_even))
    pl.when(t % 2 == 1)(lambda: step(qn_odd, qn_even, s_even, mx_even, s_odd, mx_odd))


def _attention_pipelined(q, q_norm_w, rope_tables, zb, ctx_seg, lat_seg, *, seq_q, tq, hp):
    m, width = q.shape
    n_heads = width // HEAD_DIM
    rep = n_heads // N_KV_HEADS
    assert rep % hp == 0
    tiles_per_step = rep // hp
    nqt = seq_q // tq
    n_steps = (m // seq_q) * N_KV_HEADS * nqt
    (kc, vtc), (kl, vtl) = ctx_seg, lat_seg
    n_ctx, n_lat = vtc.shape[-1], vtl.shape[-1]

    def decode(step):
        return step // (N_KV_HEADS * nqt), (step // nqt) % N_KV_HEADS, step % nqt

    def stage(lag):
        return lambda t: decode(jnp.clip(t - lag, 0, n_steps - 1))

    prep, score, finish = stage(0), stage(1), stage(2)

    def q_idx(b, g, qi):
        return (b * nqt + qi, g)

    def kv_idx(b, g, qi):
        return (b, g)

    q_block = (tq, rep * HEAD_DIM)
    n_cols = hp * tq
    rope_spec = pl.BlockSpec((tq, HEAD_DIM), lambda t: (prep(t)[2], 0))
    tile_scratch = [pltpu.VMEM((tiles_per_step, n_cols, HEAD_DIM), BF16),
                    pltpu.VMEM((tiles_per_step, n_ctx + n_lat, n_cols), F32),
                    pltpu.VMEM((tiles_per_step, 1, n_cols), F32)]
    return pl.pallas_call(
        functools.partial(_attn_pipelined_kernel, hp=hp),
        out_shape=jax.ShapeDtypeStruct((m, width), BF16),
        grid=(n_steps + 2,),
        in_specs=[
            pl.BlockSpec(q_block, lambda t: q_idx(*prep(t))),
            rope_spec,
            rope_spec,
            pl.BlockSpec((1, HEAD_DIM), lambda t: (0, 0)),
            pl.BlockSpec((n_ctx, HEAD_DIM), lambda t: kv_idx(*score(t))),
            pl.BlockSpec((n_lat, HEAD_DIM), lambda t: kv_idx(*score(t))),
            pl.BlockSpec((1, 1, HEAD_DIM, n_ctx), lambda t: kv_idx(*finish(t)) + (0, 0)),
            pl.BlockSpec((1, 1, HEAD_DIM, n_lat), lambda t: kv_idx(*finish(t)) + (0, 0)),
            pl.BlockSpec(q_block, lambda t: q_idx(*finish(t))),
        ],
        out_specs=pl.BlockSpec(q_block, lambda t: q_idx(*finish(t))),
        scratch_shapes=tile_scratch * 2
        + [pltpu.VMEM((n_ctx + n_lat, n_cols), BF16), pltpu.VMEM((HEAD_DIM, n_cols), F32)],
        compiler_params=_params("arbitrary"),
        name="attention_pipelined",
    )(q, *rope_tables, q_norm_w.reshape(1, HEAD_DIM), kc, kl, vtc, vtl, zb)


def _outproj_kernel(a_ref, b_ref, w_ref, x_ref, g_ref, o_ref, acc_scr, *, tm):
    ka = a_ref.shape[1]
    d_in, d_out = w_ref.shape

    def matmul_pieces(n, slot):
        cols = slice(n * COL_TILE, (n + 1) * COL_TILE)
        acc = []

        def piece(k0):
            src, off = (a_ref, k0) if k0 < ka else (b_ref, k0 - ka)
            part = jnp.dot(src[:, off:off + K_CHUNK], w_ref[k0:k0 + K_CHUNK, cols], preferred_element_type=F32)
            acc[:] = [part if not acc else acc[0] + part]
            if k0 + K_CHUNK == d_in:
                acc_scr[slot] = acc[0]
        return [functools.partial(piece, k0) for k0 in range(0, d_in, K_CHUNK)]

    def residual_piece(n, slot, r0):
        rows, cols = slice(r0, r0 + CHUNK), slice(n * COL_TILE, (n + 1) * COL_TILE)
        o_ref[rows, cols] = x_ref[rows, cols] + g_ref[0, 0, :, cols] * acc_scr[slot, rows, :]

    pending = []
    for n in range(d_out // COL_TILE):
        slot = n % 2
        for mm, ep in itertools.zip_longest(matmul_pieces(n, slot), pending):
            if mm is not None:
                mm()
            if ep is not None:
                ep()
        pending = [functools.partial(residual_piece, n, slot, r0) for r0 in range(0, tm, CHUNK)]
    for ep in pending:
        ep()


def _output_projection(a, b, w_out, x2d, mod, mod_rows, seq, *, tm):
    m, d = x2d.shape
    tpb = seq // tm
    return pl.pallas_call(
        functools.partial(_outproj_kernel, tm=tm),
        out_shape=jax.ShapeDtypeStruct((m, d), F32),
        grid=(m // tm,),
        in_specs=[
            pl.BlockSpec((tm, a.shape[1]), lambda i: (i, 0)),
            pl.BlockSpec((tm, b.shape[1]), lambda i: (i, 0)),
            pl.BlockSpec((None,) + w_out.shape[1:], lambda i: (mod_rows[0], 0, 0), pipeline_mode=pl.Buffered(1)),
            pl.BlockSpec((tm, d), lambda i: (i, 0)),
            _mod_spec(mod_rows, MOD_GATE, d, tpb),
        ],
        out_specs=pl.BlockSpec((tm, d), lambda i: (i, 0)),
        scratch_shapes=[pltpu.VMEM((2, tm, COL_TILE), F32)],
        compiler_params=_params("arbitrary"),
        name="output_projection",
    )(a, b, w_out, x2d, mod)


def _pair_dims(t):
    lead = t.shape[:-1]
    quarter = HEAD_DIM // 4
    t = t.reshape(*lead, -1, 2, 2, quarter)
    return jnp.swapaxes(t, -3, -2).reshape(*lead, -1)


def _rope_tables(n_tokens):
    rows = n_tokens // GRID_W
    row_id = jnp.broadcast_to(jnp.arange(rows)[:, None], (rows, GRID_W)).reshape(-1)
    col_id = jnp.broadcast_to(jnp.arange(GRID_W)[None, :], (rows, GRID_W)).reshape(-1)
    axis_dim = HEAD_DIM // 2
    inv_freq = ROPE_THETA ** (-jnp.arange(0, axis_dim, 2, dtype=F32) / axis_dim)
    ang_r = row_id.astype(F32)[:, None] * inv_freq[None, :]
    ang_c = col_id.astype(F32)[:, None] * inv_freq[None, :]
    ang = _pair_dims(jnp.concatenate([ang_r, ang_r, ang_c, ang_c], axis=-1))
    sign = jnp.where(jnp.arange(HEAD_DIM) < HEAD_DIM // 2, -1.0, 1.0)
    return jnp.cos(ang), jnp.sin(ang) * sign


def _pair_lanes(t):
    quarter = HEAD_DIM // 4
    lane = lax.broadcasted_iota(jnp.int32, t.shape, 1)
    from_right = pltpu.roll(t, HEAD_DIM - quarter, 1)
    from_left = pltpu.roll(t, quarter, 1)
    second = (lane >= quarter) & (lane < 2 * quarter)
    third = (lane >= 2 * quarter) & (lane < 3 * quarter)
    return jnp.where(second, from_right, jnp.where(third, from_left, t))


def _w_in_layout_kernel(u_ref, v_ref, za_ref, zb_ref, qkv_ref, o_ref):
    n = pl.program_id(1)

    @pl.when(n < SGU_GROUPS)
    def _():
        for i, ref in enumerate((u_ref, v_ref, za_ref, zb_ref)):
            o_ref[0, :, i * LANES:(i + 1) * LANES] = ref[0].astype(BF16)

    @pl.when(n >= SGU_GROUPS)
    def _():
        is_kv_tile = n == pl.num_programs(1) - 1
        for hh in range(COL_TILE // HEAD_DIM):
            cols = slice(hh * HEAD_DIM, (hh + 1) * HEAD_DIM)
            t = qkv_ref[0, :, cols]
            paired = _pair_lanes(t)
            if hh >= N_KV_HEADS:
                paired = jnp.where(is_kv_tile, t, paired)
            o_ref[0, :, cols] = paired.astype(BF16)


def _layout_w_in(w_in, sgu_width, attn_width):
    depth, d, d_in = w_in.shape
    g = SGU_GROUPS
    kv_width = N_KV_HEADS * HEAD_DIM
    q0 = 3 * sgu_width
    zb0 = q0 + attn_width + 2 * kv_width
    n_tiles = d_in // COL_TILE
    assert q0 % COL_TILE == 0 and zb0 % LANES == 0 and 2 * kv_width == COL_TILE

    def branch_spec(base):
        return pl.BlockSpec((1, d, LANES), lambda l, n: (l, 0, base // LANES + jnp.minimum(n, g - 1)))

    return pl.pallas_call(
        _w_in_layout_kernel,
        out_shape=jax.ShapeDtypeStruct((depth, d, d_in), BF16),
        grid=(depth, n_tiles),
        in_specs=[branch_spec(0), branch_spec(sgu_width), branch_spec(2 * sgu_width), branch_spec(zb0),
                  pl.BlockSpec((1, d, COL_TILE), lambda l, n: (l, 0, q0 // COL_TILE + jnp.maximum(n - g, 0)))],
        out_specs=pl.BlockSpec((1, d, COL_TILE), lambda l, n: (l, 0, n)),
        compiler_params=_params("arbitrary", "arbitrary"),
        name="w_in_layout",
    )(w_in, w_in, w_in, w_in, w_in)


def _cast_kernel(x_ref, o_ref):
    o_ref[...] = x_ref[...].astype(o_ref.dtype)


def _cast_bf16(w, rows=512):
    depth, r, c = w.shape
    spec = pl.BlockSpec((1, rows, c), lambda l, i: (l, i, 0))
    return pl.pallas_call(
        _cast_kernel,
        out_shape=jax.ShapeDtypeStruct(w.shape, BF16),
        grid=(depth, r // rows),
        in_specs=[spec],
        out_specs=spec,
        compiler_params=_params("arbitrary", "arbitrary"),
        name="cast_bf16",
    )(w)


def kernel(x, c, ctx, c_ctx, norm_w, w_mod, b_mod, w_in, w_sgu, b_sgu, v_norm_w, q_norm_w, k_norm_w, w_out):
    batch, seq, d = x.shape
    ctx_len = ctx.shape[1]
    depth = norm_w.shape[0]
    sgu_width = SGU_GROUPS * v_norm_w.shape[-1]
    attn_width = w_out.shape[1] - sgu_width
    assert sgu_width == SGU_GROUPS * LANES and attn_width == sgu_width
    assert w_in.shape[2] == 3 * sgu_width + 2 * attn_width + 2 * N_KV_HEADS * HEAD_DIM

    rows = 8 * pl.cdiv(batch + 1, 8)
    cond = jnp.concatenate([c, c_ctx[None], jnp.zeros((rows - batch - 1, d), F32)], axis=0)
    mod = _modulation(cond, w_mod, b_mod).reshape(depth, rows, 1, 3 * d)

    tables = _rope_tables(seq)
    xl = x.reshape(batch * seq, d)
    xc = ctx.reshape(batch * ctx_len, d)
    tm_lat, tm_ctx = 512, ctx_len
    w_in_p = _layout_w_in(w_in, sgu_width, attn_width)
    w_out_b = _cast_bf16(w_out)

    for layer in range(depth):
        last = layer == depth - 1
        lat_rows, ctx_rows = (layer, None), (layer, batch)
        sgu_params = (w_sgu[layer], b_sgu[layer], v_norm_w[layer])
        qnw, knw = _pair_dims(q_norm_w[layer]), _pair_dims(k_norm_w[layer])

        ctx_out = _input_projection(
            xc, ctx_len, norm_w[layer], mod, ctx_rows, w_in_p, knw,
            sgu_params, None, tm=tm_ctx, kv_only=last)
        kc, vtc = ctx_out[-2:]
        if not last:
            sgu_c, zb_c, q_c = ctx_out[:3]
            attn_c = _attention(q_c, qnw, zb_c, [(kc, vtc)], seq_q=ctx_len, tq=ctx_len,
                                hp=ATTN_HEADS_PER_STEP)
            xc_next = _output_projection(sgu_c, attn_c, w_out_b, xc, mod, ctx_rows, ctx_len, tm=tm_ctx)

        sgu_l, zb_l, q_l, kl, vtl = _input_projection(
            xl, seq, norm_w[layer], mod, lat_rows, w_in_p, knw,
            sgu_params, tables, tm=tm_lat, kv_only=False)
        attn_l = _attention_pipelined(q_l, qnw, tables, zb_l, (kc, vtc), (kl, vtl), seq_q=seq, tq=256,
                                      hp=ATTN_HEADS_PER_STEP)
        xl = _output_projection(sgu_l, attn_l, w_out_b, xl, mod, lat_rows, seq, tm=tm_lat)
        if not last:
            xc = xc_next
    return xl.reshape(batch, seq, d)
```

```python
import functools
import itertools
import math

import jax
import jax.numpy as jnp
from jax import lax
from jax.experimental import pallas as pl
from jax.experimental.pallas import tpu as pltpu

F32 = jnp.float32
BF16 = jnp.bfloat16

GRID_W = 64
CHUNK = 128
HEAD_DIM = 128
N_KV_HEADS = 2
SGU_GROUPS = 8
ROPE_THETA = 10000.0
EPS = 1e-6

LANES = 128
SUBLANES = 8
COL_TILE = 4 * LANES
K_CHUNK = 512
NORM_ROWS = 32
VMEM_LIMIT = 56 * 1024 * 1024
ATTN_HEADS_PER_STEP = 2
QK_ROWS = 512
SM_ROWS = 64
Q_SCALE = HEAD_DIM ** -0.5 * math.log2(math.e)


def _params(*sem):
    return pltpu.CompilerParams(dimension_semantics=sem, vmem_limit_bytes=VMEM_LIMIT)


def _rms(t, w):
    return t * lax.rsqrt(jnp.mean(t * t, axis=-1, keepdims=True) + EPS) * w


def _silu(t):
    return t * jax.nn.sigmoid(t)


def _rope(t, cos, sin_signed):
    return t * cos + pltpu.roll(t, HEAD_DIM // 2, 1) * sin_signed


def _fold_rows(x, op):
    parts = [x[i:i + SUBLANES] for i in range(0, x.shape[0], SUBLANES)]
    while len(parts) > 1:
        parts = [op(a, b) for a, b in zip(parts[0::2], parts[1::2])] + ([parts[-1]] if len(parts) % 2 else [])
    return parts[0]


def _mod_kernel(cond_ref, w_ref, b_ref, o_ref):
    a = _silu(cond_ref[...]).astype(BF16)
    o_ref[0] = jnp.dot(a, w_ref[0].astype(BF16), preferred_element_type=F32) + b_ref[0]


def _modulation(cond, w_mod, b_mod, tn=1024):
    depth, d, n = w_mod.shape
    rows = cond.shape[0]
    return pl.pallas_call(
        _mod_kernel,
        out_shape=jax.ShapeDtypeStruct((depth, rows, n), F32),
        grid=(depth, n // tn),
        in_specs=[
            pl.BlockSpec((rows, d), lambda l, j: (0, 0)),
            pl.BlockSpec((1, d, tn), lambda l, j: (l, 0, j)),
            pl.BlockSpec((1, 1, tn), lambda l, j: (l, 0, j)),
        ],
        out_specs=pl.BlockSpec((1, rows, tn), lambda l, j: (l, 0, j)),
        compiler_params=_params("arbitrary", "arbitrary"),
        name="modulation",
    )(cond, w_mod, b_mod.reshape(depth, 1, n))


def _inproj_kernel(*refs, tm, rope, kv_only):
    it = iter(refs)
    x_ref, nw_ref, scale_ref, shift_ref, w_ref, knw_ref = (next(it) for _ in range(6))
    if not kv_only:
        wsgu_ref, bsgu_ref, vnw_ref = (next(it) for _ in range(3))
    if rope:
        cos_ref, sin_ref = (next(it) for _ in range(2))
    if not kv_only:
        sgu_out, zb_out, q_out = (next(it) for _ in range(3))
    k_out, vt_out, h_scr, acc_scr = (next(it) for _ in range(4))
    if not kv_only:
        vn_scr, s_scr = (next(it) for _ in range(2))

    d = h_scr.shape[1]
    n_tiles = w_ref.shape[1] // COL_TILE

    nw, gain, shift = nw_ref[...], 1.0 + scale_ref[0, 0], shift_ref[0, 0]
    for r0 in range(0, tm, NORM_ROWS):
        rows = slice(r0, r0 + NORM_ROWS)
        h_scr[rows, :] = (_rms(x_ref[rows, :], nw) * gain + shift).astype(BF16)

    def maybe_rope(t, rows):
        return _rope(t, cos_ref[rows, :], sin_ref[rows, :]) if rope else t

    def matmul_pieces(n, slot):
        cols = slice(n * COL_TILE, (n + 1) * COL_TILE)
        acc = []

        def piece(k0):
            part = jnp.dot(h_scr[:, k0:k0 + K_CHUNK], w_ref[k0:k0 + K_CHUNK, cols], preferred_element_type=F32)
            acc[:] = [part if not acc else acc[0] + part]
            if k0 + K_CHUNK == d:
                acc_scr[slot] = acc[0]
        return [functools.partial(piece, k0) for k0 in range(0, d, K_CHUNK)]

    def acc_block(slot, rows, i):
        return acc_scr[slot, rows, i * LANES:(i + 1) * LANES]

    def sgu_norm_piece(g, slot, r0, mix):
        rows = slice(r0, r0 + CHUNK)
        vn_scr[slot, :, rows] = _rms(jax.nn.gelu(acc_block(slot, rows, 1)), vnw_ref[g]).astype(BF16)
        if mix:
            s_scr[slot] = jnp.dot(wsgu_ref[g], vn_scr[slot], preferred_element_type=F32) + bsgu_ref[g]

    def sgu_gate_piece(g, slot, r0):
        rows = slice(r0, r0 + CHUNK)
        cols = slice(g * LANES, (g + 1) * LANES)
        gated = jax.nn.gelu(acc_block(slot, rows, 0)) * _silu(acc_block(slot, rows, 2))
        sgu_out[rows, cols] = (gated * s_scr[slot, :, rows]).astype(BF16)
        zb_out[rows, cols] = _silu(acc_block(slot, rows, 3)).astype(BF16)

    def q_piece(qt, slot, r0):
        rows = slice(r0, r0 + CHUNK)
        for hh in range(COL_TILE // HEAD_DIM):
            c0 = qt * COL_TILE + hh * HEAD_DIM
            q_out[rows, c0:c0 + HEAD_DIM] = acc_block(slot, rows, hh)

    def kv_piece(slot, r0):
        rows = slice(r0, r0 + CHUNK)
        for hh in range(N_KV_HEADS):
            k = maybe_rope(_rms(acc_block(slot, rows, hh), knw_ref[...]), rows)
            k_out[rows, hh * HEAD_DIM:(hh + 1) * HEAD_DIM] = k.astype(BF16)
            vt_out[0, hh, :, rows] = acc_block(slot, rows, N_KV_HEADS + hh).T.astype(BF16)

    def epilogue_pieces(n, slot):
        row_starts = range(0, tm, CHUNK)
        if kv_only or n == n_tiles - 1:
            return [functools.partial(kv_piece, slot, r0) for r0 in row_starts]
        if n < SGU_GROUPS:
            last_r0 = row_starts[-1]
            return ([functools.partial(sgu_norm_piece, n, slot, r0, r0 == last_r0) for r0 in row_starts]
                    + [functools.partial(sgu_gate_piece, n, slot, r0) for r0 in row_starts])
        return [functools.partial(q_piece, n - SGU_GROUPS, slot, r0) for r0 in row_starts]

    def row_matmul_pieces(n, slot):
        cols = slice(n * COL_TILE, (n + 1) * COL_TILE)

        def piece(r0):
            rows = slice(r0, r0 + CHUNK)
            acc_scr[slot, rows, :] = jnp.dot(h_scr[rows, :], w_ref[:, cols], preferred_element_type=F32)
        return [functools.partial(piece, r0) for r0 in range(0, tm, CHUNK)]

    def spread(pieces, n_slots):
        bounds = [len(pieces) * i // n_slots for i in range(n_slots + 1)]
        return [pieces[a:b] for a, b in zip(bounds, bounds[1:])]

    pending = []
    for n in range(n_tiles - 1):
        slot = n % 2
        mms = matmul_pieces(n, slot)
        for mm, eps in zip(mms, spread(pending, len(mms))):
            mm()
            for ep in eps:
                ep()
        pending = epilogue_pieces(n, slot)
    n = n_tiles - 1
    slot = n % 2
    mms = row_matmul_pieces(n, slot)
    own = []
    for mm, ep, prevs in zip(mms, epilogue_pieces(n, slot), spread(pending, len(mms))):
        for prev in prevs:
            prev()
        mm()
        for piece in own:
            piece()
        own = [ep]
    for piece in own:
        piece()


MOD_SHIFT, MOD_SCALE, MOD_GATE = 0, 1, 2


def _mod_spec(mod_rows, chunk, d, tpb):
    layer, row = mod_rows
    return pl.BlockSpec((1, 1, 1, d), lambda i: (layer, (i // tpb) if row is None else row, 0, chunk))


def _input_projection(x2d, seq, norm_w, mod, mod_rows, w_in_p, k_norm_w, sgu_params,
                      rope_tables, *, tm, kv_only):
    m, d = x2d.shape
    tpb = seq // tm
    n_tiles = w_in_p.shape[2] // COL_TILE
    rope = rope_tables is not None
    g = SGU_GROUPS
    resident = pl.Buffered(1)
    layer = mod_rows[0]

    const2 = lambda i: (0, 0)
    const3 = lambda i: (0, 0, 0)
    if kv_only:
        w_spec = pl.BlockSpec((None, d, COL_TILE), lambda i: (layer, 0, n_tiles - 1), pipeline_mode=resident)
    else:
        w_spec = pl.BlockSpec((None, d, n_tiles * COL_TILE), lambda i: (layer, 0, 0), pipeline_mode=resident)
    in_specs = [
        pl.BlockSpec((tm, d), lambda i: (i, 0)),
        pl.BlockSpec((1, d), const2),
        _mod_spec(mod_rows, MOD_SCALE, d, tpb),
        _mod_spec(mod_rows, MOD_SHIFT, d, tpb),
        w_spec,
        pl.BlockSpec((1, HEAD_DIM), const2),
    ]
    args = [x2d, norm_w.reshape(1, d), mod, mod, w_in_p, k_norm_w.reshape(1, HEAD_DIM)]
    if not kv_only:
        w_sgu, b_sgu, v_norm_w = sgu_params
        in_specs += [
            pl.BlockSpec((g, CHUNK, CHUNK), const3),
            pl.BlockSpec((g, CHUNK, 1), const3),
            pl.BlockSpec((g, 1, LANES), const3),
        ]
        args += [w_sgu.astype(BF16), b_sgu.reshape(g, CHUNK, 1), v_norm_w.reshape(g, 1, LANES)]
    if rope:
        in_specs += [pl.BlockSpec((tm, HEAD_DIM), lambda i: (i % tpb, 0))] * 2
        args += list(rope_tables)

    out_shape, out_specs = [], []
    if not kv_only:
        for width, dtype in ((g * LANES, BF16), (g * LANES, BF16), ((n_tiles - g - 1) * COL_TILE, F32)):
            out_shape.append(jax.ShapeDtypeStruct((m, width), dtype))
            out_specs.append(pl.BlockSpec((tm, width), lambda i: (i, 0)))
    out_shape += [jax.ShapeDtypeStruct((m, N_KV_HEADS * HEAD_DIM), BF16),
                  jax.ShapeDtypeStruct((m // seq, N_KV_HEADS, HEAD_DIM, seq), BF16)]
    out_specs += [pl.BlockSpec((tm, N_KV_HEADS * HEAD_DIM), lambda i: (i, 0)),
                  pl.BlockSpec((1, N_KV_HEADS, HEAD_DIM, tm), lambda i: (i // tpb, 0, 0, i % tpb))]

    return pl.pallas_call(
        functools.partial(_inproj_kernel, tm=tm, rope=rope, kv_only=kv_only),
        out_shape=out_shape,
        grid=(m // tm,),
        in_specs=in_specs,
        out_specs=out_specs,
        scratch_shapes=[pltpu.VMEM((tm, d), BF16), pltpu.VMEM((2, tm, COL_TILE), F32)]
        + ([] if kv_only else [pltpu.VMEM((2, CHUNK, tm), BF16), pltpu.VMEM((2, CHUNK, tm), F32)]),
        compiler_params=_params("arbitrary"),
        name="input_projection_kv" if kv_only else "input_projection",
    )(*args)


def _prepare_q(q_raw, qnw, rope_tables):
    parts = []
    for c0 in range(0, q_raw.shape[1], HEAD_DIM):
        t = _rms(q_raw[:, c0:c0 + HEAD_DIM], qnw)
        if rope_tables is not None:
            t = _rope(t, *rope_tables)
        parts.append((t * Q_SCALE).astype(BF16))
    return jnp.concatenate(parts, axis=0)


def _attn_kernel(*refs, n_seg, hp):
    q_ref, qnw_ref = refs[:2]
    refs = refs[1:]
    k_refs = refs[1:1 + n_seg]
    vt_refs = refs[1 + n_seg:1 + 2 * n_seg]
    zb_ref, o_ref = refs[1 + 2 * n_seg:]
    nt = (((1,), (1,)), ((), ()))
    tq = q_ref.shape[0]
    heads = [slice(hh * HEAD_DIM, (hh + 1) * HEAD_DIM) for hh in range(hp)]
    q = _prepare_q(q_ref[...], qnw_ref[...], None)
    s = [lax.dot_general(k[...], q, nt, preferred_element_type=F32) for k in k_refs]
    mx = functools.reduce(jnp.maximum, [jnp.max(t, axis=0, keepdims=True) for t in s])
    p = [jnp.exp2(t - mx) for t in s]
    den = functools.reduce(jnp.add, [jnp.sum(t, axis=0, keepdims=True) for t in p])
    o_t = functools.reduce(jnp.add, [
        jnp.dot(vt[0, 0], t.astype(BF16), preferred_element_type=F32) for vt, t in zip(vt_refs, p)])
    o_t = o_t * (1.0 / den)
    for hh, sl in enumerate(heads):
        o = o_t[:, hh * tq:(hh + 1) * tq].T
        o_ref[:, sl] = (o * zb_ref[:, sl].astype(F32)).astype(BF16)


def _attention(q, q_norm_w, zb, segments, *, seq_q, tq, hp):
    m, width = q.shape
    n_heads = width // HEAD_DIM
    rep = n_heads // N_KV_HEADS
    nqt = seq_q // tq
    batch = m // seq_q
    assert rep % hp == 0
    q_spec = pl.BlockSpec((tq, hp * HEAD_DIM), lambda b, h, i: (b * nqt + i, h))
    k_specs, vt_specs, ks, vts = [], [], [], []
    for k, vt in segments:
        n = vt.shape[-1]
        k_specs.append(pl.BlockSpec((n, HEAD_DIM), lambda b, h, i: (b, (h * hp) // rep)))
        vt_specs.append(pl.BlockSpec((1, 1, HEAD_DIM, n), lambda b, h, i: (b, (h * hp) // rep, 0, 0)))
        ks.append(k)
        vts.append(vt)
    return pl.pallas_call(
        functools.partial(_attn_kernel, n_seg=len(segments), hp=hp),
        out_shape=jax.ShapeDtypeStruct((m, width), BF16),
        grid=(batch, n_heads // hp, nqt),
        in_specs=[q_spec, pl.BlockSpec((1, HEAD_DIM), lambda b, h, i: (0, 0))] + k_specs + vt_specs + [q_spec],
        out_specs=q_spec,
        compiler_params=_params("arbitrary", "arbitrary", "arbitrary"),
        name="attention",
    )(q, q_norm_w.reshape(1, HEAD_DIM), *ks, *vts, zb)


def _attn_pipelined_kernel(q_ref, cos_ref, sin_ref, qnw_ref, kc_ref, kl_ref, vtc_ref, vtl_ref, zb_ref, o_ref,
                           qn_even, s_even, mx_even, qn_odd, s_odd, mx_odd, p_scr, o_scr, *, hp):
    t = pl.program_id(0)
    nt = (((1,), (1,)), ((), ()))
    tq = q_ref.shape[0]
    n_ctx = kc_ref.shape[0]
    tiles_per_step = qn_even.shape[0]

    @pl.when(t == 0)
    def _():
        qn_odd[...] = jnp.zeros_like(qn_odd)
        s_even[...] = jnp.zeros_like(s_even)
        mx_even[...] = jnp.zeros_like(mx_even)

    def tile(ti, qn_w, qn_r, s_w, mx_w, s_r, mx_r):
        heads = [slice((ti * hp + hh) * HEAD_DIM, (ti * hp + hh + 1) * HEAD_DIM) for hh in range(hp)]
        tile_cols = slice(ti * hp * HEAD_DIM, (ti + 1) * hp * HEAD_DIM)
        s_w, mx_w, s_r, mx_r = s_w.at[ti], mx_w.at[ti], s_r.at[ti], mx_r.at[ti]
        qn_w[ti] = _prepare_q(q_ref[:, tile_cols], qnw_ref[...], (cos_ref[...], sin_ref[...]))
        q = qn_r[ti]
        mx = mx_r[...]
        mx8 = den8 = None
        chunks = [(0, kc_ref[...], vtc_ref[0, 0])] + [
            (n_ctx + r0, kl_ref[r0:r0 + QK_ROWS, :], vtl_ref[0, 0, :, r0:r0 + QK_ROWS])
            for r0 in range(0, kl_ref.shape[0], QK_ROWS)]

        def score_chunk(ci):
            nonlocal mx8
            r0, k, _ = chunks[ci]
            sc = lax.dot_general(k, q, nt, preferred_element_type=F32)
            s_w[r0:r0 + k.shape[0], :] = sc
            m8 = _fold_rows(sc, jnp.maximum)
            mx8 = m8 if mx8 is None else jnp.maximum(mx8, m8)

        last = len(chunks) - 1
        for ci, (r0, k, vt) in enumerate(chunks):
            if ci < last:
                score_chunk(ci)
            rows = k.shape[0]
            for r1 in range(r0, r0 + rows, SM_ROWS):
                pc = jnp.exp2(s_r[r1:r1 + SM_ROWS, :] - mx)
                d8 = _fold_rows(pc, jnp.add)
                den8 = d8 if den8 is None else den8 + d8
                p_scr[r1:r1 + SM_ROWS, :] = pc.astype(BF16)
            pv = jnp.dot(vt, p_scr[r0:r0 + rows, :], preferred_element_type=F32)
            if ci == 0:
                o_scr[...] = pv
            else:
                o_scr[...] += pv
        den = jnp.sum(den8, axis=0, keepdims=True)
        o_t = o_scr[...] * (1.0 / den)
        for hh, sl in enumerate(heads):
            o = o_t[:, hh * tq:(hh + 1) * tq].T
            o_ref[:, sl] = (o * zb_ref[:, sl].astype(F32)).astype(BF16)
        score_chunk(last)
        mx_w[...] = jnp.max(mx8, axis=0, keepdims=True)

    def step(*buffers):
        for ti in range(tiles_per_step):
            pl.when(t + ti >= 0)(functools.partial(tile, ti, *buffers))

    pl.when(t % 2 == 0)(lambda: step(qn_even, qn_odd, s_odd, mx_odd, s_even, mx_even))
    pl.when(t % 2 == 1)(lambda: step(qn_odd, qn_even, s_even, mx_even, s_odd, mx_odd))


def _attention_pipelined(q, q_norm_w, rope_tables, zb, ctx_seg, lat_seg, *, seq_q, tq, hp):
    m, width = q.shape
    n_heads = width // HEAD_DIM
    rep = n_heads // N_KV_HEADS
    assert rep % hp == 0
    tiles_per_step = rep // hp
    nqt = seq_q // tq
    n_steps = (m // seq_q) * N_KV_HEADS * nqt
    (kc, vtc), (kl, vtl) = ctx_seg, lat_seg
    n_ctx, n_lat = vtc.shape[-1], vtl.shape[-1]

    def decode(step):
        return step // (N_KV_HEADS * nqt), (step // nqt) % N_KV_HEADS, step % nqt

    def stage(lag):
        return lambda t: decode(jnp.clip(t - lag, 0, n_steps - 1))

    prep, score, finish = stage(0), stage(1), stage(2)

    def q_idx(b, g, qi):
        return (b * nqt + qi, g)

    def kv_idx(b, g, qi):
        return (b, g)

    q_block = (tq, rep * HEAD_DIM)
    n_cols = hp * tq
    rope_spec = pl.BlockSpec((tq, HEAD_DIM), lambda t: (prep(t)[2], 0))
    tile_scratch = [pltpu.VMEM((tiles_per_step, n_cols, HEAD_DIM), BF16),
                    pltpu.VMEM((tiles_per_step, n_ctx + n_lat, n_cols), F32),
                    pltpu.VMEM((tiles_per_step, 1, n_cols), F32)]
    return pl.pallas_call(
        functools.partial(_attn_pipelined_kernel, hp=hp),
        out_shape=jax.ShapeDtypeStruct((m, width), BF16),
        grid=(n_steps + 2,),
        in_specs=[
            pl.BlockSpec(q_block, lambda t: q_idx(*prep(t))),
            rope_spec,
            rope_spec,
            pl.BlockSpec((1, HEAD_DIM), lambda t: (0, 0)),
            pl.BlockSpec((n_ctx, HEAD_DIM), lambda t: kv_idx(*score(t))),
            pl.BlockSpec((n_lat, HEAD_DIM), lambda t: kv_idx(*score(t))),
            pl.BlockSpec((1, 1, HEAD_DIM, n_ctx), lambda t: kv_idx(*finish(t)) + (0, 0)),
            pl.BlockSpec((1, 1, HEAD_DIM, n_lat), lambda t: kv_idx(*finish(t)) + (0, 0)),
            pl.BlockSpec(q_block, lambda t: q_idx(*finish(t))),
        ],
        out_specs=pl.BlockSpec(q_block, lambda t: q_idx(*finish(t))),
        scratch_shapes=tile_scratch * 2
        + [pltpu.VMEM((n_ctx + n_lat, n_cols), BF16), pltpu.VMEM((HEAD_DIM, n_cols), F32)],
        compiler_params=_params("arbitrary"),
        name="attention_pipelined",
    )(q, *rope_tables, q_norm_w.reshape(1, HEAD_DIM), kc, kl, vtc, vtl, zb)


def _outproj_kernel(a_ref, b_ref, w_ref, x_ref, g_ref, o_ref, acc_scr, *, tm):
    ka = a_ref.shape[1]
    d_in, d_out = w_ref.shape

    def matmul_pieces(n, slot):
        cols = slice(n * COL_TILE, (n + 1) * COL_TILE)
        acc = []

        def piece(k0):
            src, off = (a_ref, k0) if k0 < ka else (b_ref, k0 - ka)
            w = w_ref[k0:k0 + K_CHUNK, cols].astype(BF16)
            part = jnp.dot(src[:, off:off + K_CHUNK], w, preferred_element_type=F32)
            acc[:] = [part if not acc else acc[0] + part]
            if k0 + K_CHUNK == d_in:
                acc_scr[slot] = acc[0]
        return [functools.partial(piece, k0) for k0 in range(0, d_in, K_CHUNK)]

    def residual_piece(n, slot, r0):
        rows, cols = slice(r0, r0 + CHUNK), slice(n * COL_TILE, (n + 1) * COL_TILE)
        o_ref[rows, cols] = x_ref[rows, cols] + g_ref[0, 0, :, cols] * acc_scr[slot, rows, :]

    pending = []
    for n in range(d_out // COL_TILE):
        slot = n % 2
        for mm, ep in itertools.zip_longest(matmul_pieces(n, slot), pending):
            if mm is not None:
                mm()
            if ep is not None:
                ep()
        pending = [functools.partial(residual_piece, n, slot, r0) for r0 in range(0, tm, CHUNK)]
    for ep in pending:
        ep()


def _output_projection(a, b, w_out, x2d, mod, mod_rows, seq, *, tm):
    m, d = x2d.shape
    tpb = seq // tm
    return pl.pallas_call(
        functools.partial(_outproj_kernel, tm=tm),
        out_shape=jax.ShapeDtypeStruct((m, d), F32),
        grid=(m // tm,),
        in_specs=[
            pl.BlockSpec((tm, a.shape[1]), lambda i: (i, 0)),
            pl.BlockSpec((tm, b.shape[1]), lambda i: (i, 0)),
            pl.BlockSpec((None,) + w_out.shape[1:], lambda i: (mod_rows[0], 0, 0), pipeline_mode=pl.Buffered(1)),
            pl.BlockSpec((tm, d), lambda i: (i, 0)),
            _mod_spec(mod_rows, MOD_GATE, d, tpb),
        ],
        out_specs=pl.BlockSpec((tm, d), lambda i: (i, 0)),
        scratch_shapes=[pltpu.VMEM((2, tm, COL_TILE), F32)],
        compiler_params=_params("arbitrary"),
        name="output_projection",
    )(a, b, w_out, x2d, mod)


def _pair_dims(t):
    lead = t.shape[:-1]
    quarter = HEAD_DIM // 4
    t = t.reshape(*lead, -1, 2, 2, quarter)
    return jnp.swapaxes(t, -3, -2).reshape(*lead, -1)


def _rope_tables(n_tokens):
    rows = n_tokens // GRID_W
    row_id = jnp.broadcast_to(jnp.arange(rows)[:, None], (rows, GRID_W)).reshape(-1)
    col_id = jnp.broadcast_to(jnp.arange(GRID_W)[None, :], (rows, GRID_W)).reshape(-1)
    axis_dim = HEAD_DIM // 2
    inv_freq = ROPE_THETA ** (-jnp.arange(0, axis_dim, 2, dtype=F32) / axis_dim)
    ang_r = row_id.astype(F32)[:, None] * inv_freq[None, :]
    ang_c = col_id.astype(F32)[:, None] * inv_freq[None, :]
    ang = _pair_dims(jnp.concatenate([ang_r, ang_r, ang_c, ang_c], axis=-1))
    sign = jnp.where(jnp.arange(HEAD_DIM) < HEAD_DIM // 2, -1.0, 1.0)
    return jnp.cos(ang), jnp.sin(ang) * sign


def _pair_lanes(t):
    quarter = HEAD_DIM // 4
    lane = lax.broadcasted_iota(jnp.int32, t.shape, 1)
    from_right = pltpu.roll(t, HEAD_DIM - quarter, 1)
    from_left = pltpu.roll(t, quarter, 1)
    second = (lane >= quarter) & (lane < 2 * quarter)
    third = (lane >= 2 * quarter) & (lane < 3 * quarter)
    return jnp.where(second, from_right, jnp.where(third, from_left, t))


def _w_in_layout_kernel(u_ref, v_ref, za_ref, zb_ref, qkv_ref, o_ref):
    n = pl.program_id(1)

    @pl.when(n < SGU_GROUPS)
    def _():
        for i, ref in enumerate((u_ref, v_ref, za_ref, zb_ref)):
            o_ref[0, :, i * LANES:(i + 1) * LANES] = ref[0].astype(BF16)

    @pl.when(n >= SGU_GROUPS)
    def _():
        is_kv_tile = n == pl.num_programs(1) - 1
        for hh in range(COL_TILE // HEAD_DIM):
            cols = slice(hh * HEAD_DIM, (hh + 1) * HEAD_DIM)
            t = qkv_ref[0, :, cols]
            paired = _pair_lanes(t)
            if hh >= N_KV_HEADS:
                paired = jnp.where(is_kv_tile, t, paired)
            o_ref[0, :, cols] = paired.astype(BF16)


def _layout_w_in(w_in, sgu_width, attn_width):
    depth, d, d_in = w_in.shape
    g = SGU_GROUPS
    kv_width = N_KV_HEADS * HEAD_DIM
    q0 = 3 * sgu_width
    zb0 = q0 + attn_width + 2 * kv_width
    n_tiles = d_in // COL_TILE
    assert q0 % COL_TILE == 0 and zb0 % LANES == 0 and 2 * kv_width == COL_TILE

    def branch_spec(base):
        return pl.BlockSpec((1, d, LANES), lambda l, n: (l, 0, base // LANES + jnp.minimum(n, g - 1)))

    return pl.pallas_call(
        _w_in_layout_kernel,
        out_shape=jax.ShapeDtypeStruct((depth, d, d_in), BF16),
        grid=(depth, n_tiles),
        in_specs=[branch_spec(0), branch_spec(sgu_width), branch_spec(2 * sgu_width), branch_spec(zb0),
                  pl.BlockSpec((1, d, COL_TILE), lambda l, n: (l, 0, q0 // COL_TILE + jnp.maximum(n - g, 0)))],
        out_specs=pl.BlockSpec((1, d, COL_TILE), lambda l, n: (l, 0, n)),
        compiler_params=_params("arbitrary", "arbitrary"),
        name="w_in_layout",
    )(w_in, w_in, w_in, w_in, w_in)


def kernel(x, c, ctx, c_ctx, norm_w, w_mod, b_mod, w_in, w_sgu, b_sgu, v_norm_w, q_norm_w, k_norm_w, w_out):
    batch, seq, d = x.shape
    ctx_len = ctx.shape[1]
    depth = norm_w.shape[0]
    sgu_width = SGU_GROUPS * v_norm_w.shape[-1]
    attn_width = w_out.shape[1] - sgu_width
    assert sgu_width == SGU_GROUPS * LANES and attn_width == sgu_width
    assert w_in.shape[2] == 3 * sgu_width + 2 * attn_width + 2 * N_KV_HEADS * HEAD_DIM

    rows = 8 * pl.cdiv(batch + 1, 8)
    cond = jnp.concatenate([c, c_ctx[None], jnp.zeros((rows - batch - 1, d), F32)], axis=0)
    mod = _modulation(cond, w_mod, b_mod).reshape(depth, rows, 1, 3 * d)

    tables = _rope_tables(seq)
    xl = x.reshape(batch * seq, d)
    xc = ctx.reshape(batch * ctx_len, d)
    tm_lat, tm_ctx = 512, ctx_len
    w_in_p = _layout_w_in(w_in, sgu_width, attn_width)
    w_out_b = w_out

    for layer in range(depth):
        last = layer == depth - 1
        lat_rows, ctx_rows = (layer, None), (layer, batch)
        sgu_params = (w_sgu[layer], b_sgu[layer], v_norm_w[layer])
        qnw, knw = _pair_dims(q_norm_w[layer]), _pair_dims(k_norm_w[layer])

        ctx_out = _input_projection(
            xc, ctx_len, norm_w[layer], mod, ctx_rows, w_in_p, knw,
            sgu_params, None, tm=tm_ctx, kv_only=last)
        kc, vtc = ctx_out[-2:]
        if not last:
            sgu_c, zb_c, q_c = ctx_out[:3]
            attn_c = _attention(q_c, qnw, zb_c, [(kc, vtc)], seq_q=ctx_len, tq=ctx_len,
                                hp=ATTN_HEADS_PER_STEP)
            xc_next = _output_projection(sgu_c, attn_c, w_out_b, xc, mod, ctx_rows, ctx_len, tm=tm_ctx)

        sgu_l, zb_l, q_l, kl, vtl = _input_projection(
            xl, seq, norm_w[layer], mod, lat_rows, w_in_p, knw,
            sgu_params, tables, tm=tm_lat, kv_only=False)
        attn_l = _attention_pipelined(q_l, qnw, tables, zb_l, (kc, vtc), (kl, vtl), seq_q=seq, tq=256,
                                      hp=ATTN_HEADS_PER_STEP)
        xl = _output_projection(sgu_l, attn_l, w_out_b, xl, mod, lat_rows, seq, tm=tm_lat)
        if not last:
            xc = xc_next
    return xl.reshape(batch, seq, d)
```

```python
import functools
import itertools
import math

import jax
import jax.numpy as jnp
from jax import lax
from jax.experimental import pallas as pl
from jax.experimental.pallas import tpu as pltpu

F32 = jnp.float32
BF16 = jnp.bfloat16

GRID_W = 64
CHUNK = 128
HEAD_DIM = 128
N_KV_HEADS = 2
SGU_GROUPS = 8
ROPE_THETA = 10000.0
EPS = 1e-6

LANES = 128
SUBLANES = 8
COL_TILE = 4 * LANES
K_CHUNK = 512
NORM_ROWS = 32
VMEM_LIMIT = 56 * 1024 * 1024
ATTN_HEADS_PER_STEP = 2
QK_ROWS = 256
SM_ROWS = 64
Q_SCALE = HEAD_DIM ** -0.5 * math.log2(math.e)


def _params(*sem):
    return pltpu.CompilerParams(dimension_semantics=sem, vmem_limit_bytes=VMEM_LIMIT)


def _rms(t, w):
    return t * lax.rsqrt(jnp.mean(t * t, axis=-1, keepdims=True) + EPS) * w


def _silu(t):
    return t * jax.nn.sigmoid(t)


def _rope(t, cos, sin_signed):
    return t * cos + pltpu.roll(t, HEAD_DIM // 2, 1) * sin_signed


def _fold_rows(x, op):
    parts = [x[i:i + SUBLANES] for i in range(0, x.shape[0], SUBLANES)]
    while len(parts) > 1:
        parts = [op(a, b) for a, b in zip(parts[0::2], parts[1::2])] + ([parts[-1]] if len(parts) % 2 else [])
    return parts[0]


def _mod_kernel(cond_ref, w_ref, b_ref, o_ref):
    a = _silu(cond_ref[...]).astype(BF16)
    o_ref[0] = jnp.dot(a, w_ref[0].astype(BF16), preferred_element_type=F32) + b_ref[0]


def _modulation(cond, w_mod, b_mod, tn=1024):
    depth, d, n = w_mod.shape
    rows = cond.shape[0]
    return pl.pallas_call(
        _mod_kernel,
        out_shape=jax.ShapeDtypeStruct((depth, rows, n), F32),
        grid=(depth, n // tn),
        in_specs=[
            pl.BlockSpec((rows, d), lambda l, j: (0, 0)),
            pl.BlockSpec((1, d, tn), lambda l, j: (l, 0, j)),
            pl.BlockSpec((1, 1, tn), lambda l, j: (l, 0, j)),
        ],
        out_specs=pl.BlockSpec((1, rows, tn), lambda l, j: (l, 0, j)),
        compiler_params=_params("arbitrary", "arbitrary"),
        name="modulation",
    )(cond, w_mod, b_mod.reshape(depth, 1, n))


def _inproj_kernel(*refs, tm, rope, kv_only):
    it = iter(refs)
    x_ref, nw_ref, scale_ref, shift_ref, w_ref, knw_ref = (next(it) for _ in range(6))
    if not kv_only:
        wsgu_ref, bsgu_ref, vnw_ref = (next(it) for _ in range(3))
    if rope:
        cos_ref, sin_ref = (next(it) for _ in range(2))
    if not kv_only:
        sgu_out, zb_out, q_out = (next(it) for _ in range(3))
    k_out, vt_out, h_scr, acc_scr = (next(it) for _ in range(4))
    if not kv_only:
        vn_scr, s_scr = (next(it) for _ in range(2))

    d = h_scr.shape[1]
    n_tiles = w_ref.shape[1] // COL_TILE

    nw, gain, shift = nw_ref[...], 1.0 + scale_ref[0, 0], shift_ref[0, 0]
    for r0 in range(0, tm, NORM_ROWS):
        rows = slice(r0, r0 + NORM_ROWS)
        h_scr[rows, :] = (_rms(x_ref[rows, :], nw) * gain + shift).astype(BF16)

    def maybe_rope(t, rows):
        return _rope(t, cos_ref[rows, :], sin_ref[rows, :]) if rope else t

    def matmul_pieces(n, slot):
        cols = slice(n * COL_TILE, (n + 1) * COL_TILE)
        acc = []

        def piece(k0):
            part = jnp.dot(h_scr[:, k0:k0 + K_CHUNK], w_ref[k0:k0 + K_CHUNK, cols], preferred_element_type=F32)
            acc[:] = [part if not acc else acc[0] + part]
            if k0 + K_CHUNK == d:
                acc_scr[slot] = acc[0]
        return [functools.partial(piece, k0) for k0 in range(0, d, K_CHUNK)]

    def acc_block(slot, rows, i):
        return acc_scr[slot, rows, i * LANES:(i + 1) * LANES]

    def sgu_norm_piece(g, slot, r0, mix):
        rows = slice(r0, r0 + CHUNK)
        vn_scr[slot, :, rows] = _rms(jax.nn.gelu(acc_block(slot, rows, 1)), vnw_ref[g]).astype(BF16)
        if mix:
            s_scr[slot] = jnp.dot(wsgu_ref[g], vn_scr[slot], preferred_element_type=F32) + bsgu_ref[g]

    def sgu_gate_piece(g, slot, r0):
        rows = slice(r0, r0 + CHUNK)
        cols = slice(g * LANES, (g + 1) * LANES)
        gated = jax.nn.gelu(acc_block(slot, rows, 0)) * _silu(acc_block(slot, rows, 2))
        sgu_out[rows, cols] = (gated * s_scr[slot, :, rows]).astype(BF16)
        zb_out[rows, cols] = _silu(acc_block(slot, rows, 3)).astype(BF16)

    def q_piece(qt, slot, r0):
        rows = slice(r0, r0 + CHUNK)
        for hh in range(COL_TILE // HEAD_DIM):
            c0 = qt * COL_TILE + hh * HEAD_DIM
            q_out[rows, c0:c0 + HEAD_DIM] = acc_block(slot, rows, hh)

    def kv_piece(slot, r0):
        rows = slice(r0, r0 + CHUNK)
        for hh in range(N_KV_HEADS):
            k = maybe_rope(_rms(acc_block(slot, rows, hh), knw_ref[...]), rows)
            k_out[rows, hh * HEAD_DIM:(hh + 1) * HEAD_DIM] = k.astype(BF16)
            vt_out[0, hh, :, rows] = acc_block(slot, rows, N_KV_HEADS + hh).T.astype(BF16)

    def epilogue_pieces(n, slot):
        row_starts = range(0, tm, CHUNK)
        if kv_only or n == n_tiles - 1:
            return [functools.partial(kv_piece, slot, r0) for r0 in row_starts]
        if n < SGU_GROUPS:
            last_r0 = row_starts[-1]
            return ([functools.partial(sgu_norm_piece, n, slot, r0, r0 == last_r0) for r0 in row_starts]
                    + [functools.partial(sgu_gate_piece, n, slot, r0) for r0 in row_starts])
        return [functools.partial(q_piece, n - SGU_GROUPS, slot, r0) for r0 in row_starts]

    def row_matmul_pieces(n, slot):
        cols = slice(n * COL_TILE, (n + 1) * COL_TILE)

        def piece(r0):
            rows = slice(r0, r0 + CHUNK)
            acc_scr[slot, rows, :] = jnp.dot(h_scr[rows, :], w_ref[:, cols], preferred_element_type=F32)
        return [functools.partial(piece, r0) for r0 in range(0, tm, CHUNK)]

    def spread(pieces, n_slots):
        bounds = [len(pieces) * i // n_slots for i in range(n_slots + 1)]
        return [pieces[a:b] for a, b in zip(bounds, bounds[1:])]

    pending = []
    for n in range(n_tiles - 1):
        slot = n % 2
        mms = matmul_pieces(n, slot)
        for mm, eps in zip(mms, spread(pending, len(mms))):
            mm()
            for ep in eps:
                ep()
        pending = epilogue_pieces(n, slot)
    n = n_tiles - 1
    slot = n % 2
    mms = row_matmul_pieces(n, slot)
    own = []
    for mm, ep, prevs in zip(mms, epilogue_pieces(n, slot), spread(pending, len(mms))):
        for prev in prevs:
            prev()
        mm()
        for piece in own:
            piece()
        own = [ep]
    for piece in own:
        piece()


MOD_SHIFT, MOD_SCALE, MOD_GATE = 0, 1, 2


def _mod_spec(mod_rows, chunk, d, tpb):
    layer, row = mod_rows
    return pl.BlockSpec((1, 1, 1, d), lambda i: (layer, (i // tpb) if row is None else row, 0, chunk))


def _input_projection(x2d, seq, norm_w, mod, mod_rows, w_in_p, k_norm_w, sgu_params,
                      rope_tables, *, tm, kv_only):
    m, d = x2d.shape
    tpb = seq // tm
    n_tiles = w_in_p.shape[2] // COL_TILE
    rope = rope_tables is not None
    g = SGU_GROUPS
    resident = pl.Buffered(1)
    layer = mod_rows[0]

    const2 = lambda i: (0, 0)
    const3 = lambda i: (0, 0, 0)
    if kv_only:
        w_spec = pl.BlockSpec((None, d, COL_TILE), lambda i: (layer, 0, n_tiles - 1), pipeline_mode=resident)
    else:
        w_spec = pl.BlockSpec((None, d, n_tiles * COL_TILE), lambda i: (layer, 0, 0), pipeline_mode=resident)
    in_specs = [
        pl.BlockSpec((tm, d), lambda i: (i, 0)),
        pl.BlockSpec((1, d), const2),
        _mod_spec(mod_rows, MOD_SCALE, d, tpb),
        _mod_spec(mod_rows, MOD_SHIFT, d, tpb),
        w_spec,
        pl.BlockSpec((1, HEAD_DIM), const2),
    ]
    args = [x2d, norm_w.reshape(1, d), mod, mod, w_in_p, k_norm_w.reshape(1, HEAD_DIM)]
    if not kv_only:
        w_sgu, b_sgu, v_norm_w = sgu_params
        in_specs += [
            pl.BlockSpec((g, CHUNK, CHUNK), const3),
            pl.BlockSpec((g, CHUNK, 1), const3),
            pl.BlockSpec((g, 1, LANES), const3),
        ]
        args += [w_sgu.astype(BF16), b_sgu.reshape(g, CHUNK, 1), v_norm_w.reshape(g, 1, LANES)]
    if rope:
        in_specs += [pl.BlockSpec((tm, HEAD_DIM), lambda i: (i % tpb, 0))] * 2
        args += list(rope_tables)

    out_shape, out_specs = [], []
    if not kv_only:
        for width, dtype in ((g * LANES, BF16), (g * LANES, BF16), ((n_tiles - g - 1) * COL_TILE, F32)):
            out_shape.append(jax.ShapeDtypeStruct((m, width), dtype))
            out_specs.append(pl.BlockSpec((tm, width), lambda i: (i, 0)))
    out_shape += [jax.ShapeDtypeStruct((m, N_KV_HEADS * HEAD_DIM), BF16),
                  jax.ShapeDtypeStruct((m // seq, N_KV_HEADS, HEAD_DIM, seq), BF16)]
    out_specs += [pl.BlockSpec((tm, N_KV_HEADS * HEAD_DIM), lambda i: (i, 0)),
                  pl.BlockSpec((1, N_KV_HEADS, HEAD_DIM, tm), lambda i: (i // tpb, 0, 0, i % tpb))]

    return pl.pallas_call(
        functools.partial(_inproj_kernel, tm=tm, rope=rope, kv_only=kv_only),
        out_shape=out_shape,
        grid=(m // tm,),
        in_specs=in_specs,
        out_specs=out_specs,
        scratch_shapes=[pltpu.VMEM((tm, d), BF16), pltpu.VMEM((2, tm, COL_TILE), F32)]
        + ([] if kv_only else [pltpu.VMEM((2, CHUNK, tm), BF16), pltpu.VMEM((2, CHUNK, tm), F32)]),
        compiler_params=_params("arbitrary"),
        name="input_projection_kv" if kv_only else "input_projection",
    )(*args)


def _prepare_q(q_raw, qnw, rope_tables):
    parts = []
    for c0 in range(0, q_raw.shape[1], HEAD_DIM):
        t = _rms(q_raw[:, c0:c0 + HEAD_DIM], qnw)
        if rope_tables is not None:
            t = _rope(t, *rope_tables)
        parts.append((t * Q_SCALE).astype(BF16))
    return jnp.concatenate(parts, axis=0)


def _attn_kernel(*refs, n_seg, hp):
    q_ref, qnw_ref = refs[:2]
    refs = refs[1:]
    k_refs = refs[1:1 + n_seg]
    vt_refs = refs[1 + n_seg:1 + 2 * n_seg]
    zb_ref, o_ref = refs[1 + 2 * n_seg:]
    nt = (((1,), (1,)), ((), ()))
    tq = q_ref.shape[0]
    heads = [slice(hh * HEAD_DIM, (hh + 1) * HEAD_DIM) for hh in range(hp)]
    q = _prepare_q(q_ref[...], qnw_ref[...], None)
    s = [lax.dot_general(k[...], q, nt, preferred_element_type=F32) for k in k_refs]
    mx = functools.reduce(jnp.maximum, [jnp.max(t, axis=0, keepdims=True) for t in s])
    p = [jnp.exp2(t - mx) for t in s]
    den = functools.reduce(jnp.add, [jnp.sum(t, axis=0, keepdims=True) for t in p])
    o_t = functools.reduce(jnp.add, [
        jnp.dot(vt[0, 0], t.astype(BF16), preferred_element_type=F32) for vt, t in zip(vt_refs, p)])
    o_t = o_t * (1.0 / den)
    for hh, sl in enumerate(heads):
        o = o_t[:, hh * tq:(hh + 1) * tq].T
        o_ref[:, sl] = (o * zb_ref[:, sl].astype(F32)).astype(BF16)


def _attention(q, q_norm_w, zb, segments, *, seq_q, tq, hp):
    m, width = q.shape
    n_heads = width // HEAD_DIM
    rep = n_heads // N_KV_HEADS
    nqt = seq_q // tq
    batch = m // seq_q
    assert rep % hp == 0
    q_spec = pl.BlockSpec((tq, hp * HEAD_DIM), lambda b, h, i: (b * nqt + i, h))
    k_specs, vt_specs, ks, vts = [], [], [], []
    for k, vt in segments:
        n = vt.shape[-1]
        k_specs.append(pl.BlockSpec((n, HEAD_DIM), lambda b, h, i: (b, (h * hp) // rep)))
        vt_specs.append(pl.BlockSpec((1, 1, HEAD_DIM, n), lambda b, h, i: (b, (h * hp) // rep, 0, 0)))
        ks.append(k)
        vts.append(vt)
    return pl.pallas_call(
        functools.partial(_attn_kernel, n_seg=len(segments), hp=hp),
        out_shape=jax.ShapeDtypeStruct((m, width), BF16),
        grid=(batch, n_heads // hp, nqt),
        in_specs=[q_spec, pl.BlockSpec((1, HEAD_DIM), lambda b, h, i: (0, 0))] + k_specs + vt_specs + [q_spec],
        out_specs=q_spec,
        compiler_params=_params("arbitrary", "arbitrary", "arbitrary"),
        name="attention",
    )(q, q_norm_w.reshape(1, HEAD_DIM), *ks, *vts, zb)


def _attn_pipelined_kernel(q_ref, cos_ref, sin_ref, qnw_ref, kc_ref, kl_ref, vtc_ref, vtl_ref, zb_ref, o_ref,
                           qn_even, s_even, mx_even, acc_even, den_even,
                           qn_odd, s_odd, mx_odd, acc_odd, den_odd, p_scr, *, hp):
    t = pl.program_id(0)
    nt = (((1,), (1,)), ((), ()))
    tq = q_ref.shape[0]
    n_ctx = kc_ref.shape[0]
    tiles_per_step = qn_even.shape[0]

    @pl.when(t == 0)
    def _():
        qn_odd[...] = jnp.zeros_like(qn_odd)
        s_even[...] = jnp.zeros_like(s_even)
        mx_even[...] = jnp.zeros_like(mx_even)
        acc_odd[...] = jnp.zeros_like(acc_odd)
        den_odd[...] = jnp.ones_like(den_odd)

    def tile(ti, qn_w, qn_r, s_w, mx_w, s_r, mx_r, acc_w, den_w, acc_r, den_r):
        heads = [slice((ti * hp + hh) * HEAD_DIM, (ti * hp + hh + 1) * HEAD_DIM) for hh in range(hp)]
        tile_cols = slice(ti * hp * HEAD_DIM, (ti + 1) * hp * HEAD_DIM)
        s_w, mx_w, s_r, mx_r, acc_w = s_w.at[ti], mx_w.at[ti], s_r.at[ti], mx_r.at[ti], acc_w.at[ti]
        o_t = acc_r[ti] * (1.0 / den_r[ti])
        for hh, sl in enumerate(heads):
            o = o_t[:, hh * tq:(hh + 1) * tq].T
            o_ref[:, sl] = (o * zb_ref[:, sl].astype(F32)).astype(BF16)
        qn_w[ti] = _prepare_q(q_ref[:, tile_cols], qnw_ref[...], (cos_ref[...], sin_ref[...]))
        q = qn_r[ti]
        mx = mx_r[...]
        mx8 = den8 = None
        chunks = [(0, kc_ref[...], vtc_ref[0, 0])] + [
            (n_ctx + r0, kl_ref[r0:r0 + QK_ROWS, :], vtl_ref[0, 0, :, r0:r0 + QK_ROWS])
            for r0 in range(0, kl_ref.shape[0], QK_ROWS)]

        def score_chunk(ci):
            nonlocal mx8
            r0, k, _ = chunks[ci]
            sc = lax.dot_general(k, q, nt, preferred_element_type=F32)
            s_w[r0:r0 + k.shape[0], :] = sc
            m8 = _fold_rows(sc, jnp.maximum)
            mx8 = m8 if mx8 is None else jnp.maximum(mx8, m8)

        for ci, (r0, k, vt) in enumerate(chunks):
            score_chunk(ci)
            rows = k.shape[0]
            for r1 in range(r0, r0 + rows, SM_ROWS):
                pc = jnp.exp2(s_r[r1:r1 + SM_ROWS, :] - mx)
                d8 = _fold_rows(pc, jnp.add)
                den8 = d8 if den8 is None else den8 + d8
                p_scr[r1:r1 + SM_ROWS, :] = pc.astype(BF16)
            pv = jnp.dot(vt, p_scr[r0:r0 + rows, :], preferred_element_type=F32)
            if ci == 0:
                acc_w[...] = pv
            else:
                acc_w[...] += pv
        den_w[ti] = jnp.sum(den8, axis=0, keepdims=True)
        mx_w[...] = jnp.max(mx8, axis=0, keepdims=True)

    def step(*buffers):
        for ti in range(tiles_per_step):
            pl.when(t + ti >= 0)(functools.partial(tile, ti, *buffers))

    even = (qn_even, s_even, mx_even, acc_even, den_even)
    odd = (qn_odd, s_odd, mx_odd, acc_odd, den_odd)

    def run(cur, oth):
        step(cur[0], oth[0], oth[1], oth[2], cur[1], cur[2], cur[3], cur[4], oth[3], oth[4])

    pl.when(t % 2 == 0)(lambda: run(even, odd))
    pl.when(t % 2 == 1)(lambda: run(odd, even))


def _attention_pipelined(q, q_norm_w, rope_tables, zb, ctx_seg, lat_seg, *, seq_q, tq, hp):
    m, width = q.shape
    n_heads = width // HEAD_DIM
    rep = n_heads // N_KV_HEADS
    assert rep % hp == 0
    tiles_per_step = rep // hp
    nqt = seq_q // tq
    n_steps = (m // seq_q) * N_KV_HEADS * nqt
    (kc, vtc), (kl, vtl) = ctx_seg, lat_seg
    n_ctx, n_lat = vtc.shape[-1], vtl.shape[-1]

    def decode(step):
        return step // (N_KV_HEADS * nqt), (step // nqt) % N_KV_HEADS, step % nqt

    def stage(lag):
        return lambda t: decode(jnp.clip(t - lag, 0, n_steps - 1))

    prep, score, finish, emit = stage(0), stage(1), stage(2), stage(3)

    def q_idx(b, g, qi):
        return (b * nqt + qi, g)

    def kv_idx(b, g, qi):
        return (b, g)

    q_block = (tq, rep * HEAD_DIM)
    n_cols = hp * tq
    rope_spec = pl.BlockSpec((tq, HEAD_DIM), lambda t: (prep(t)[2], 0))
    tile_scratch = [pltpu.VMEM((tiles_per_step, n_cols, HEAD_DIM), BF16),
                    pltpu.VMEM((tiles_per_step, n_ctx + n_lat, n_cols), F32),
                    pltpu.VMEM((tiles_per_step, 1, n_cols), F32),
                    pltpu.VMEM((tiles_per_step, HEAD_DIM, n_cols), F32),
                    pltpu.VMEM((tiles_per_step, 1, n_cols), F32)]
    return pl.pallas_call(
        functools.partial(_attn_pipelined_kernel, hp=hp),
        out_shape=jax.ShapeDtypeStruct((m, width), BF16),
        grid=(n_steps + 3,),
        in_specs=[
            pl.BlockSpec(q_block, lambda t: q_idx(*prep(t))),
            rope_spec,
            rope_spec,
            pl.BlockSpec((1, HEAD_DIM), lambda t: (0, 0)),
            pl.BlockSpec((n_ctx, HEAD_DIM), lambda t: kv_idx(*score(t))),
            pl.BlockSpec((n_lat, HEAD_DIM), lambda t: kv_idx(*score(t))),
            pl.BlockSpec((1, 1, HEAD_DIM, n_ctx), lambda t: kv_idx(*finish(t)) + (0, 0)),
            pl.BlockSpec((1, 1, HEAD_DIM, n_lat), lambda t: kv_idx(*finish(t)) + (0, 0)),
            pl.BlockSpec(q_block, lambda t: q_idx(*emit(t))),
        ],
        out_specs=pl.BlockSpec(q_block, lambda t: q_idx(*emit(t))),
        scratch_shapes=tile_scratch * 2 + [pltpu.VMEM((n_ctx + n_lat, n_cols), BF16)],
        compiler_params=_params("arbitrary"),
        name="attention_pipelined",
    )(q, *rope_tables, q_norm_w.reshape(1, HEAD_DIM), kc, kl, vtc, vtl, zb)


def _outproj_kernel(a_ref, b_ref, w_ref, x_ref, g_ref, o_ref, acc_scr, *, tm):
    ka = a_ref.shape[1]
    d_in, d_out = w_ref.shape

    def matmul_pieces(n, slot):
        cols = slice(n * COL_TILE, (n + 1) * COL_TILE)
        acc = []

        def piece(k0):
            src, off = (a_ref, k0) if k0 < ka else (b_ref, k0 - ka)
            w = w_ref[k0:k0 + K_CHUNK, cols].astype(BF16)
            part = jnp.dot(src[:, off:off + K_CHUNK], w, preferred_element_type=F32)
            acc[:] = [part if not acc else acc[0] + part]
            if k0 + K_CHUNK == d_in:
                acc_scr[slot] = acc[0]
        return [functools.partial(piece, k0) for k0 in range(0, d_in, K_CHUNK)]

    def residual_piece(n, slot, r0):
        rows, cols = slice(r0, r0 + CHUNK), slice(n * COL_TILE, (n + 1) * COL_TILE)
        o_ref[rows, cols] = x_ref[rows, cols] + g_ref[0, 0, :, cols] * acc_scr[slot, rows, :]

    pending = []
    for n in range(d_out // COL_TILE):
        slot = n % 2
        for mm, ep in itertools.zip_longest(matmul_pieces(n, slot), pending):
            if mm is not None:
                mm()
            if ep is not None:
                ep()
        pending = [functools.partial(residual_piece, n, slot, r0) for r0 in range(0, tm, CHUNK)]
    for ep in pending:
        ep()


def _output_projection(a, b, w_out, x2d, mod, mod_rows, seq, *, tm):
    m, d = x2d.shape
    tpb = seq // tm
    return pl.pallas_call(
        functools.partial(_outproj_kernel, tm=tm),
        out_shape=jax.ShapeDtypeStruct((m, d), F32),
        grid=(m // tm,),
        in_specs=[
            pl.BlockSpec((tm, a.shape[1]), lambda i: (i, 0)),
            pl.BlockSpec((tm, b.shape[1]), lambda i: (i, 0)),
            pl.BlockSpec((None,) + w_out.shape[1:], lambda i: (mod_rows[0], 0, 0), pipeline_mode=pl.Buffered(1)),
            pl.BlockSpec((tm, d), lambda i: (i, 0)),
            _mod_spec(mod_rows, MOD_GATE, d, tpb),
        ],
        out_specs=pl.BlockSpec((tm, d), lambda i: (i, 0)),
        scratch_shapes=[pltpu.VMEM((2, tm, COL_TILE), F32)],
        compiler_params=_params("arbitrary"),
        name="output_projection",
    )(a, b, w_out, x2d, mod)


def _pair_dims(t):
    lead = t.shape[:-1]
    quarter = HEAD_DIM // 4
    t = t.reshape(*lead, -1, 2, 2, quarter)
    return jnp.swapaxes(t, -3, -2).reshape(*lead, -1)


def _rope_tables(n_tokens):
    rows = n_tokens // GRID_W
    row_id = jnp.broadcast_to(jnp.arange(rows)[:, None], (rows, GRID_W)).reshape(-1)
    col_id = jnp.broadcast_to(jnp.arange(GRID_W)[None, :], (rows, GRID_W)).reshape(-1)
    axis_dim = HEAD_DIM // 2
    inv_freq = ROPE_THETA ** (-jnp.arange(0, axis_dim, 2, dtype=F32) / axis_dim)
    ang_r = row_id.astype(F32)[:, None] * inv_freq[None, :]
    ang_c = col_id.astype(F32)[:, None] * inv_freq[None, :]
    ang = _pair_dims(jnp.concatenate([ang_r, ang_r, ang_c, ang_c], axis=-1))
    sign = jnp.where(jnp.arange(HEAD_DIM) < HEAD_DIM // 2, -1.0, 1.0)
    return jnp.cos(ang), jnp.sin(ang) * sign


def _pair_lanes(t):
    quarter = HEAD_DIM // 4
    lane = lax.broadcasted_iota(jnp.int32, t.shape, 1)
    from_right = pltpu.roll(t, HEAD_DIM - quarter, 1)
    from_left = pltpu.roll(t, quarter, 1)
    second = (lane >= quarter) & (lane < 2 * quarter)
    third = (lane >= 2 * quarter) & (lane < 3 * quarter)
    return jnp.where(second, from_right, jnp.where(third, from_left, t))


def _w_in_layout_kernel(u_ref, v_ref, za_ref, zb_ref, qkv_ref, o_ref):
    n = pl.program_id(1)

    @pl.when(n < SGU_GROUPS)
    def _():
        for i, ref in enumerate((u_ref, v_ref, za_ref, zb_ref)):
            o_ref[0, :, i * LANES:(i + 1) * LANES] = ref[0].astype(BF16)

    @pl.when(n >= SGU_GROUPS)
    def _():
        is_kv_tile = n == pl.num_programs(1) - 1
        for hh in range(COL_TILE // HEAD_DIM):
            cols = slice(hh * HEAD_DIM, (hh + 1) * HEAD_DIM)
            t = qkv_ref[0, :, cols]
            paired = _pair_lanes(t)
            if hh >= N_KV_HEADS:
                paired = jnp.where(is_kv_tile, t, paired)
            o_ref[0, :, cols] = paired.astype(BF16)


def _layout_w_in(w_in, sgu_width, attn_width):
    depth, d, d_in = w_in.shape
    g = SGU_GROUPS
    kv_width = N_KV_HEADS * HEAD_DIM
    q0 = 3 * sgu_width
    zb0 = q0 + attn_width + 2 * kv_width
    n_tiles = d_in // COL_TILE
    assert q0 % COL_TILE == 0 and zb0 % LANES == 0 and 2 * kv_width == COL_TILE

    def branch_spec(base):
        return pl.BlockSpec((1, d, LANES), lambda l, n: (l, 0, base // LANES + jnp.minimum(n, g - 1)))

    return pl.pallas_call(
        _w_in_layout_kernel,
        out_shape=jax.ShapeDtypeStruct((depth, d, d_in), BF16),
        grid=(depth, n_tiles),
        in_specs=[branch_spec(0), branch_spec(sgu_width), branch_spec(2 * sgu_width), branch_spec(zb0),
                  pl.BlockSpec((1, d, COL_TILE), lambda l, n: (l, 0, q0 // COL_TILE + jnp.maximum(n - g, 0)))],
        out_specs=pl.BlockSpec((1, d, COL_TILE), lambda l, n: (l, 0, n)),
        compiler_params=_params("arbitrary", "arbitrary"),
        name="w_in_layout",
    )(w_in, w_in, w_in, w_in, w_in)


def kernel(x, c, ctx, c_ctx, norm_w, w_mod, b_mod, w_in, w_sgu, b_sgu, v_norm_w, q_norm_w, k_norm_w, w_out):
    batch, seq, d = x.shape
    ctx_len = ctx.shape[1]
    depth = norm_w.shape[0]
    sgu_width = SGU_GROUPS * v_norm_w.shape[-1]
    attn_width = w_out.shape[1] - sgu_width
    assert sgu_width == SGU_GROUPS * LANES and attn_width == sgu_width
    assert w_in.shape[2] == 3 * sgu_width + 2 * attn_width + 2 * N_KV_HEADS * HEAD_DIM

    rows = 8 * pl.cdiv(batch + 1, 8)
    cond = jnp.concatenate([c, c_ctx[None], jnp.zeros((rows - batch - 1, d), F32)], axis=0)
    mod = _modulation(cond, w_mod, b_mod).reshape(depth, rows, 1, 3 * d)

    tables = _rope_tables(seq)
    xl = x.reshape(batch * seq, d)
    xc = ctx.reshape(batch * ctx_len, d)
    tm_lat, tm_ctx = 512, ctx_len
    w_in_p = _layout_w_in(w_in, sgu_width, attn_width)
    w_out_b = w_out

    for layer in range(depth):
        last = layer == depth - 1
        lat_rows, ctx_rows = (layer, None), (layer, batch)
        sgu_params = (w_sgu[layer], b_sgu[layer], v_norm_w[layer])
        qnw, knw = _pair_dims(q_norm_w[layer]), _pair_dims(k_norm_w[layer])

        ctx_out = _input_projection(
            xc, ctx_len, norm_w[layer], mod, ctx_rows, w_in_p, knw,
            sgu_params, None, tm=tm_ctx, kv_only=last)
        kc, vtc = ctx_out[-2:]
        if not last:
            sgu_c, zb_c, q_c = ctx_out[:3]
            attn_c = _attention(q_c, qnw, zb_c, [(kc, vtc)], seq_q=ctx_len, tq=ctx_len,
                                hp=ATTN_HEADS_PER_STEP)
            xc_next = _output_projection(sgu_c, attn_c, w_out_b, xc, mod, ctx_rows, ctx_len, tm=tm_ctx)

        sgu_l, zb_l, q_l, kl, vtl = _input_projection(
            xl, seq, norm_w[layer], mod, lat_rows, w_in_p, knw,
            sgu_params, tables, tm=tm_lat, kv_only=False)
        attn_l = _attention_pipelined(q_l, qnw, tables, zb_l, (kc, vtc), (kl, vtl), seq_q=seq, tq=256,
                                      hp=ATTN_HEADS_PER_STEP)
        xl = _output_projection(sgu_l, attn_l, w_out_b, xl, mod, lat_rows, seq, tm=tm_lat)
        if not last:
            xc = xc_next
    return xl.reshape(batch, seq, d)
```

```python
import functools
import itertools
import math

import jax
import jax.numpy as jnp
from jax import lax
from jax.experimental import pallas as pl
from jax.experimental.pallas import tpu as pltpu

F32 = jnp.float32
BF16 = jnp.bfloat16

GRID_W = 64
CHUNK = 128
HEAD_DIM = 128
N_KV_HEADS = 2
SGU_GROUPS = 8
ROPE_THETA = 10000.0
EPS = 1e-6

LANES = 128
SUBLANES = 8
COL_TILE = 4 * LANES
K_CHUNK = 512
NORM_ROWS = 32
VMEM_LIMIT = 56 * 1024 * 1024
ATTN_HEADS_PER_STEP = 2
QK_ROWS = 256
SM_ROWS = 64
Q_SCALE = HEAD_DIM ** -0.5 * math.log2(math.e)


def _params(*sem):
    return pltpu.CompilerParams(dimension_semantics=sem, vmem_limit_bytes=VMEM_LIMIT)


def _rms(t, w):
    return t * lax.rsqrt(jnp.mean(t * t, axis=-1, keepdims=True) + EPS) * w


def _silu(t):
    return t * jax.nn.sigmoid(t)


def _rope(t, cos, sin_signed):
    return t * cos + pltpu.roll(t, HEAD_DIM // 2, 1) * sin_signed


def _fold_rows(x, op):
    parts = [x[i:i + SUBLANES] for i in range(0, x.shape[0], SUBLANES)]
    while len(parts) > 1:
        parts = [op(a, b) for a, b in zip(parts[0::2], parts[1::2])] + ([parts[-1]] if len(parts) % 2 else [])
    return parts[0]


def _mod_kernel(cond_ref, w_ref, b_ref, o_ref):
    a = _silu(cond_ref[...]).astype(BF16)
    o_ref[0] = jnp.dot(a, w_ref[0].astype(BF16), preferred_element_type=F32) + b_ref[0]


def _modulation(cond, w_mod, b_mod, tn=1024):
    depth, d, n = w_mod.shape
    rows = cond.shape[0]
    return pl.pallas_call(
        _mod_kernel,
        out_shape=jax.ShapeDtypeStruct((depth, rows, n), F32),
        grid=(depth, n // tn),
        in_specs=[
            pl.BlockSpec((rows, d), lambda l, j: (0, 0)),
            pl.BlockSpec((1, d, tn), lambda l, j: (l, 0, j)),
            pl.BlockSpec((1, 1, tn), lambda l, j: (l, 0, j)),
        ],
        out_specs=pl.BlockSpec((1, rows, tn), lambda l, j: (l, 0, j)),
        compiler_params=_params("arbitrary", "arbitrary"),
        name="modulation",
    )(cond, w_mod, b_mod.reshape(depth, 1, n))


def _inproj_kernel(*refs, tm, rope, kv_only):
    it = iter(refs)
    x_ref, nw_ref, scale_ref, shift_ref, w_ref, knw_ref = (next(it) for _ in range(6))
    if not kv_only:
        wsgu_ref, bsgu_ref, vnw_ref = (next(it) for _ in range(3))
    if rope:
        cos_ref, sin_ref = (next(it) for _ in range(2))
    if not kv_only:
        sgu_out, zb_out, q_out = (next(it) for _ in range(3))
    k_out, vt_out, h_scr, acc_scr = (next(it) for _ in range(4))
    if not kv_only:
        vn_scr, s_scr = (next(it) for _ in range(2))

    d = h_scr.shape[1]
    n_tiles = w_ref.shape[1] // COL_TILE

    nw, gain, shift = nw_ref[...], 1.0 + scale_ref[0, 0], shift_ref[0, 0]
    for r0 in range(0, tm, NORM_ROWS):
        rows = slice(r0, r0 + NORM_ROWS)
        h_scr[rows, :] = (_rms(x_ref[rows, :], nw) * gain + shift).astype(BF16)

    def maybe_rope(t, rows):
        return _rope(t, cos_ref[rows, :], sin_ref[rows, :]) if rope else t

    def matmul_pieces(n, slot):
        cols = slice(n * COL_TILE, (n + 1) * COL_TILE)
        acc = []

        def piece(k0):
            part = jnp.dot(h_scr[:, k0:k0 + K_CHUNK], w_ref[k0:k0 + K_CHUNK, cols], preferred_element_type=F32)
            acc[:] = [part if not acc else acc[0] + part]
            if k0 + K_CHUNK == d:
                acc_scr[slot] = acc[0]
        return [functools.partial(piece, k0) for k0 in range(0, d, K_CHUNK)]

    def acc_block(slot, rows, i):
        return acc_scr[slot, rows, i * LANES:(i + 1) * LANES]

    def sgu_norm_piece(g, slot, r0, mix):
        rows = slice(r0, r0 + CHUNK)
        vn_scr[slot, :, rows] = _rms(jax.nn.gelu(acc_block(slot, rows, 1)), vnw_ref[g]).astype(BF16)
        if mix:
            s_scr[slot] = jnp.dot(wsgu_ref[g], vn_scr[slot], preferred_element_type=F32) + bsgu_ref[g]

    def sgu_gate_piece(g, slot, r0):
        rows = slice(r0, r0 + CHUNK)
        cols = slice(g * LANES, (g + 1) * LANES)
        gated = jax.nn.gelu(acc_block(slot, rows, 0)) * _silu(acc_block(slot, rows, 2))
        sgu_out[rows, cols] = (gated * s_scr[slot, :, rows]).astype(BF16)
        zb_out[rows, cols] = _silu(acc_block(slot, rows, 3)).astype(BF16)

    def q_piece(qt, slot, r0):
        rows = slice(r0, r0 + CHUNK)
        for hh in range(COL_TILE // HEAD_DIM):
            c0 = qt * COL_TILE + hh * HEAD_DIM
            q_out[rows, c0:c0 + HEAD_DIM] = acc_block(slot, rows, hh)

    def kv_piece(slot, r0):
        rows = slice(r0, r0 + CHUNK)
        for hh in range(N_KV_HEADS):
            k = maybe_rope(_rms(acc_block(slot, rows, hh), knw_ref[...]), rows)
            k_out[rows, hh * HEAD_DIM:(hh + 1) * HEAD_DIM] = k.astype(BF16)
            vt_out[0, hh, :, rows] = acc_block(slot, rows, N_KV_HEADS + hh).T.astype(BF16)

    def epilogue_pieces(n, slot):
        row_starts = range(0, tm, CHUNK)
        if kv_only or n == n_tiles - 1:
            return [functools.partial(kv_piece, slot, r0) for r0 in row_starts]
        if n < SGU_GROUPS:
            last_r0 = row_starts[-1]
            return ([functools.partial(sgu_norm_piece, n, slot, r0, r0 == last_r0) for r0 in row_starts]
                    + [functools.partial(sgu_gate_piece, n, slot, r0) for r0 in row_starts])
        return [functools.partial(q_piece, n - SGU_GROUPS, slot, r0) for r0 in row_starts]

    def row_matmul_pieces(n, slot):
        cols = slice(n * COL_TILE, (n + 1) * COL_TILE)

        def piece(r0):
            rows = slice(r0, r0 + CHUNK)
            acc_scr[slot, rows, :] = jnp.dot(h_scr[rows, :], w_ref[:, cols], preferred_element_type=F32)
        return [functools.partial(piece, r0) for r0 in range(0, tm, CHUNK)]

    def spread(pieces, n_slots):
        bounds = [len(pieces) * i // n_slots for i in range(n_slots + 1)]
        return [pieces[a:b] for a, b in zip(bounds, bounds[1:])]

    pending = []
    for n in range(n_tiles - 1):
        slot = n % 2
        mms = matmul_pieces(n, slot)
        for mm, eps in zip(mms, spread(pending, len(mms))):
            mm()
            for ep in eps:
                ep()
        pending = epilogue_pieces(n, slot)
    n = n_tiles - 1
    slot = n % 2
    mms = row_matmul_pieces(n, slot)
    own = []
    for mm, ep, prevs in zip(mms, epilogue_pieces(n, slot), spread(pending, len(mms))):
        for prev in prevs:
            prev()
        mm()
        for piece in own:
            piece()
        own = [ep]
    for piece in own:
        piece()


MOD_SHIFT, MOD_SCALE, MOD_GATE = 0, 1, 2


def _mod_spec(mod_rows, chunk, d, tpb):
    layer, row = mod_rows
    return pl.BlockSpec((1, 1, 1, d), lambda i: (layer, (i // tpb) if row is None else row, 0, chunk))


def _input_projection(x2d, seq, norm_w, mod, mod_rows, w_in_p, k_norm_w, sgu_params,
                      rope_tables, *, tm, kv_only):
    m, d = x2d.shape
    tpb = seq // tm
    w_in_p, layer = w_in_p
    n_tiles = w_in_p.shape[2] // COL_TILE
    rope = rope_tables is not None
    g = SGU_GROUPS
    resident = pl.Buffered(1)

    const2 = lambda i: (0, 0)
    const3 = lambda i: (0, 0, 0)
    if kv_only:
        w_spec = pl.BlockSpec((None, d, COL_TILE), lambda i: (layer, 0, n_tiles - 1), pipeline_mode=resident)
    else:
        w_spec = pl.BlockSpec((None, d, n_tiles * COL_TILE), lambda i: (layer, 0, 0), pipeline_mode=resident)
    in_specs = [
        pl.BlockSpec((tm, d), lambda i: (i, 0)),
        pl.BlockSpec((1, d), const2),
        _mod_spec(mod_rows, MOD_SCALE, d, tpb),
        _mod_spec(mod_rows, MOD_SHIFT, d, tpb),
        w_spec,
        pl.BlockSpec((1, HEAD_DIM), const2),
    ]
    args = [x2d, norm_w.reshape(1, d), mod, mod, w_in_p, k_norm_w.reshape(1, HEAD_DIM)]
    if not kv_only:
        w_sgu, b_sgu, v_norm_w = sgu_params
        in_specs += [
            pl.BlockSpec((g, CHUNK, CHUNK), const3),
            pl.BlockSpec((g, CHUNK, 1), const3),
            pl.BlockSpec((g, 1, LANES), const3),
        ]
        args += [w_sgu.astype(BF16), b_sgu.reshape(g, CHUNK, 1), v_norm_w.reshape(g, 1, LANES)]
    if rope:
        in_specs += [pl.BlockSpec((tm, HEAD_DIM), lambda i: (i % tpb, 0))] * 2
        args += list(rope_tables)

    out_shape, out_specs = [], []
    if not kv_only:
        for width, dtype in ((g * LANES, BF16), (g * LANES, BF16), ((n_tiles - g - 1) * COL_TILE, F32)):
            out_shape.append(jax.ShapeDtypeStruct((m, width), dtype))
            out_specs.append(pl.BlockSpec((tm, width), lambda i: (i, 0)))
    out_shape += [jax.ShapeDtypeStruct((m, N_KV_HEADS * HEAD_DIM), BF16),
                  jax.ShapeDtypeStruct((m // seq, N_KV_HEADS, HEAD_DIM, seq), BF16)]
    out_specs += [pl.BlockSpec((tm, N_KV_HEADS * HEAD_DIM), lambda i: (i, 0)),
                  pl.BlockSpec((1, N_KV_HEADS, HEAD_DIM, tm), lambda i: (i // tpb, 0, 0, i % tpb))]

    return pl.pallas_call(
        functools.partial(_inproj_kernel, tm=tm, rope=rope, kv_only=kv_only),
        out_shape=out_shape,
        grid=(m // tm,),
        in_specs=in_specs,
        out_specs=out_specs,
        scratch_shapes=[pltpu.VMEM((tm, d), BF16), pltpu.VMEM((2, tm, COL_TILE), F32)]
        + ([] if kv_only else [pltpu.VMEM((2, CHUNK, tm), BF16), pltpu.VMEM((2, CHUNK, tm), F32)]),
        compiler_params=_params("arbitrary"),
        name="input_projection_kv" if kv_only else "input_projection",
    )(*args)


def _prepare_q(q_raw, qnw, rope_tables):
    parts = []
    for c0 in range(0, q_raw.shape[1], HEAD_DIM):
        t = _rms(q_raw[:, c0:c0 + HEAD_DIM], qnw)
        if rope_tables is not None:
            t = _rope(t, *rope_tables)
        parts.append((t * Q_SCALE).astype(BF16))
    return jnp.concatenate(parts, axis=0)


def _attn_kernel(*refs, n_seg, hp):
    q_ref, qnw_ref = refs[:2]
    refs = refs[1:]
    k_refs = refs[1:1 + n_seg]
    vt_refs = refs[1 + n_seg:1 + 2 * n_seg]
    zb_ref, o_ref = refs[1 + 2 * n_seg:]
    nt = (((1,), (1,)), ((), ()))
    tq = q_ref.shape[0]
    heads = [slice(hh * HEAD_DIM, (hh + 1) * HEAD_DIM) for hh in range(hp)]
    q = _prepare_q(q_ref[...], qnw_ref[...], None)
    s = [lax.dot_general(k[...], q, nt, preferred_element_type=F32) for k in k_refs]
    mx = functools.reduce(jnp.maximum, [jnp.max(t, axis=0, keepdims=True) for t in s])
    p = [jnp.exp2(t - mx) for t in s]
    den = functools.reduce(jnp.add, [jnp.sum(t, axis=0, keepdims=True) for t in p])
    o_t = functools.reduce(jnp.add, [
        jnp.dot(vt[0, 0], t.astype(BF16), preferred_element_type=F32) for vt, t in zip(vt_refs, p)])
    o_t = o_t * (1.0 / den)
    for hh, sl in enumerate(heads):
        o = o_t[:, hh * tq:(hh + 1) * tq].T
        o_ref[:, sl] = (o * zb_ref[:, sl].astype(F32)).astype(BF16)


def _attention(q, q_norm_w, zb, segments, *, seq_q, tq, hp):
    m, width = q.shape
    n_heads = width // HEAD_DIM
    rep = n_heads // N_KV_HEADS
    nqt = seq_q // tq
    batch = m // seq_q
    assert rep % hp == 0
    q_spec = pl.BlockSpec((tq, hp * HEAD_DIM), lambda b, h, i: (b * nqt + i, h))
    k_specs, vt_specs, ks, vts = [], [], [], []
    for k, vt in segments:
        n = vt.shape[-1]
        k_specs.append(pl.BlockSpec((n, HEAD_DIM), lambda b, h, i: (b, (h * hp) // rep)))
        vt_specs.append(pl.BlockSpec((1, 1, HEAD_DIM, n), lambda b, h, i: (b, (h * hp) // rep, 0, 0)))
        ks.append(k)
        vts.append(vt)
    return pl.pallas_call(
        functools.partial(_attn_kernel, n_seg=len(segments), hp=hp),
        out_shape=jax.ShapeDtypeStruct((m, width), BF16),
        grid=(batch, n_heads // hp, nqt),
        in_specs=[q_spec, pl.BlockSpec((1, HEAD_DIM), lambda b, h, i: (0, 0))] + k_specs + vt_specs + [q_spec],
        out_specs=q_spec,
        compiler_params=_params("arbitrary", "arbitrary", "arbitrary"),
        name="attention",
    )(q, q_norm_w.reshape(1, HEAD_DIM), *ks, *vts, zb)


def _attn_pipelined_kernel(q_ref, cos_ref, sin_ref, qnw_ref, kc_ref, kl_ref, vtc_ref, vtl_ref, zb_ref, o_ref,
                           qn_even, s_even, mx_even, acc_even, den_even,
                           qn_odd, s_odd, mx_odd, acc_odd, den_odd, p_scr, *, hp):
    t = pl.program_id(0)
    nt = (((1,), (1,)), ((), ()))
    tq = q_ref.shape[0]
    n_ctx = kc_ref.shape[0]
    tiles_per_step = qn_even.shape[0]

    @pl.when(t == 0)
    def _():
        qn_odd[...] = jnp.zeros_like(qn_odd)
        s_even[...] = jnp.zeros_like(s_even)
        mx_even[...] = jnp.zeros_like(mx_even)
        acc_odd[...] = jnp.zeros_like(acc_odd)
        den_odd[...] = jnp.ones_like(den_odd)

    def tile(ti, qn_w, qn_r, s_w, mx_w, s_r, mx_r, acc_w, den_w, acc_r, den_r):
        heads = [slice((ti * hp + hh) * HEAD_DIM, (ti * hp + hh + 1) * HEAD_DIM) for hh in range(hp)]
        tile_cols = slice(ti * hp * HEAD_DIM, (ti + 1) * hp * HEAD_DIM)
        s_w, mx_w, s_r, mx_r, acc_w = s_w.at[ti], mx_w.at[ti], s_r.at[ti], mx_r.at[ti], acc_w.at[ti]
        o_t = acc_r[ti] * (1.0 / den_r[ti])
        for hh, sl in enumerate(heads):
            o = o_t[:, hh * tq:(hh + 1) * tq].T
            o_ref[:, sl] = (o * zb_ref[:, sl].astype(F32)).astype(BF16)
        qn_w[ti] = _prepare_q(q_ref[:, tile_cols], qnw_ref[...], (cos_ref[...], sin_ref[...]))
        q = qn_r[ti]
        mx = mx_r[...]
        mx8 = den8 = None
        chunks = [(0, kc_ref[...], vtc_ref[0, 0])] + [
            (n_ctx + r0, kl_ref[r0:r0 + QK_ROWS, :], vtl_ref[0, 0, :, r0:r0 + QK_ROWS])
            for r0 in range(0, kl_ref.shape[0], QK_ROWS)]

        def score_chunk(ci):
            nonlocal mx8
            r0, k, _ = chunks[ci]
            sc = lax.dot_general(k, q, nt, preferred_element_type=F32)
            s_w[r0:r0 + k.shape[0], :] = sc
            m8 = _fold_rows(sc, jnp.maximum)
            mx8 = m8 if mx8 is None else jnp.maximum(mx8, m8)

        for ci, (r0, k, vt) in enumerate(chunks):
            score_chunk(ci)
            rows = k.shape[0]
            for r1 in range(r0, r0 + rows, SM_ROWS):
                pc = jnp.exp2(s_r[r1:r1 + SM_ROWS, :] - mx)
                d8 = _fold_rows(pc, jnp.add)
                den8 = d8 if den8 is None else den8 + d8
                p_scr[r1:r1 + SM_ROWS, :] = pc.astype(BF16)
            pv = jnp.dot(vt, p_scr[r0:r0 + rows, :], preferred_element_type=F32)
            if ci == 0:
                acc_w[...] = pv
            else:
                acc_w[...] += pv
        den_w[ti] = jnp.sum(den8, axis=0, keepdims=True)
        mx_w[...] = jnp.max(mx8, axis=0, keepdims=True)

    def step(*buffers):
        for ti in range(tiles_per_step):
            pl.when(t + ti >= 0)(functools.partial(tile, ti, *buffers))

    even = (qn_even, s_even, mx_even, acc_even, den_even)
    odd = (qn_odd, s_odd, mx_odd, acc_odd, den_odd)

    def run(cur, oth):
        step(cur[0], oth[0], oth[1], oth[2], cur[1], cur[2], cur[3], cur[4], oth[3], oth[4])

    pl.when(t % 2 == 0)(lambda: run(even, odd))
    pl.when(t % 2 == 1)(lambda: run(odd, even))


def _attention_pipelined(q, q_norm_w, rope_tables, zb, ctx_seg, lat_seg, *, seq_q, tq, hp):
    m, width = q.shape
    n_heads = width // HEAD_DIM
    rep = n_heads // N_KV_HEADS
    assert rep % hp == 0
    tiles_per_step = rep // hp
    nqt = seq_q // tq
    n_steps = (m // seq_q) * N_KV_HEADS * nqt
    (kc, vtc), (kl, vtl) = ctx_seg, lat_seg
    n_ctx, n_lat = vtc.shape[-1], vtl.shape[-1]

    def decode(step):
        return step // (N_KV_HEADS * nqt), (step // nqt) % N_KV_HEADS, step % nqt

    def stage(lag):
        return lambda t: decode(jnp.clip(t - lag, 0, n_steps - 1))

    prep, score, finish, emit = stage(0), stage(1), stage(2), stage(3)

    def q_idx(b, g, qi):
        return (b * nqt + qi, g)

    def kv_idx(b, g, qi):
        return (b, g)

    q_block = (tq, rep * HEAD_DIM)
    n_cols = hp * tq
    rope_spec = pl.BlockSpec((tq, HEAD_DIM), lambda t: (prep(t)[2], 0))
    tile_scratch = [pltpu.VMEM((tiles_per_step, n_cols, HEAD_DIM), BF16),
                    pltpu.VMEM((tiles_per_step, n_ctx + n_lat, n_cols), F32),
                    pltpu.VMEM((tiles_per_step, 1, n_cols), F32),
                    pltpu.VMEM((tiles_per_step, HEAD_DIM, n_cols), F32),
                    pltpu.VMEM((tiles_per_step, 1, n_cols), F32)]
    return pl.pallas_call(
        functools.partial(_attn_pipelined_kernel, hp=hp),
        out_shape=jax.ShapeDtypeStruct((m, width), BF16),
        grid=(n_steps + 3,),
        in_specs=[
            pl.BlockSpec(q_block, lambda t: q_idx(*prep(t))),
            rope_spec,
            rope_spec,
            pl.BlockSpec((1, HEAD_DIM), lambda t: (0, 0)),
            pl.BlockSpec((n_ctx, HEAD_DIM), lambda t: kv_idx(*score(t))),
            pl.BlockSpec((n_lat, HEAD_DIM), lambda t: kv_idx(*score(t))),
            pl.BlockSpec((1, 1, HEAD_DIM, n_ctx), lambda t: kv_idx(*finish(t)) + (0, 0)),
            pl.BlockSpec((1, 1, HEAD_DIM, n_lat), lambda t: kv_idx(*finish(t)) + (0, 0)),
            pl.BlockSpec(q_block, lambda t: q_idx(*emit(t))),
        ],
        out_specs=pl.BlockSpec(q_block, lambda t: q_idx(*emit(t))),
        scratch_shapes=tile_scratch * 2 + [pltpu.VMEM((n_ctx + n_lat, n_cols), BF16)],
        compiler_params=_params("arbitrary"),
        name="attention_pipelined",
    )(q, *rope_tables, q_norm_w.reshape(1, HEAD_DIM), kc, kl, vtc, vtl, zb)


def _outproj_kernel(*refs, tm, next_layout):
    if next_layout is None:
        a_ref, b_ref, w_ref, x_ref, g_ref, o_ref, acc_scr = refs
    else:
        a_ref, b_ref, w_ref, x_ref, g_ref, wn_ref, o_ref, wn_out, acc_scr = refs
    ka = a_ref.shape[1]
    d_in, d_out = w_ref.shape

    def layout_piece(dst, src, paired):
        t = wn_ref[:, src * LANES:(src + 1) * LANES]
        wn_out[:, dst * LANES:(dst + 1) * LANES] = (_pair_lanes(t) if paired else t).astype(BF16)

    layout_pieces = [functools.partial(layout_piece, dst, src, paired)
                     for dst, (src, paired) in enumerate(next_layout or [])]

    def matmul_pieces(n, slot):
        cols = slice(n * COL_TILE, (n + 1) * COL_TILE)
        acc = []

        def piece(k0):
            src, off = (a_ref, k0) if k0 < ka else (b_ref, k0 - ka)
            w = w_ref[k0:k0 + K_CHUNK, cols].astype(BF16)
            part = jnp.dot(src[:, off:off + K_CHUNK], w, preferred_element_type=F32)
            acc[:] = [part if not acc else acc[0] + part]
            if k0 + K_CHUNK == d_in:
                acc_scr[slot] = acc[0]
        return [functools.partial(piece, k0) for k0 in range(0, d_in, K_CHUNK)]

    def residual_piece(n, slot, r0):
        rows, cols = slice(r0, r0 + CHUNK), slice(n * COL_TILE, (n + 1) * COL_TILE)
        o_ref[rows, cols] = x_ref[rows, cols] + g_ref[0, 0, :, cols] * acc_scr[slot, rows, :]

    n_col_tiles = d_out // COL_TILE
    layout_groups = [layout_pieces[len(layout_pieces) * i // n_col_tiles:len(layout_pieces) * (i + 1) // n_col_tiles]
                     for i in range(n_col_tiles)]
    pending = []
    for n in range(n_col_tiles):
        slot = n % 2
        fillers = pending + layout_groups[n]
        mms = matmul_pieces(n, slot)
        bounds = [len(fillers) * i // len(mms) for i in range(len(mms) + 1)]
        for mm, lo, hi in zip(mms, bounds, bounds[1:]):
            mm()
            for piece in fillers[lo:hi]:
                piece()
        pending = [functools.partial(residual_piece, n, slot, r0) for r0 in range(0, tm, CHUNK)]
    for ep in pending:
        ep()


def _w_in_block_order(sgu_width, attn_width):
    g = SGU_GROUPS
    q0 = 3 * sgu_width // LANES
    k0 = q0 + attn_width // LANES
    v0 = k0 + N_KV_HEADS
    zb0 = v0 + N_KV_HEADS
    order = []
    for grp in range(g):
        order += [(grp, False), (g + grp, False), (2 * g + grp, False), (zb0 + grp, False)]
    order += [(blk, True) for blk in range(q0, v0)]
    order += [(blk, False) for blk in range(v0, zb0)]
    return order


def _output_projection(a, b, w_out, x2d, mod, mod_rows, seq, *, tm, next_w_in=None, next_layout=None):
    m, d = x2d.shape
    tpb = seq // tm
    layer = mod_rows[0]
    in_specs = [
        pl.BlockSpec((tm, a.shape[1]), lambda i: (i, 0)),
        pl.BlockSpec((tm, b.shape[1]), lambda i: (i, 0)),
        pl.BlockSpec((None,) + w_out.shape[1:], lambda i: (layer, 0, 0), pipeline_mode=pl.Buffered(1)),
        pl.BlockSpec((tm, d), lambda i: (i, 0)),
        _mod_spec(mod_rows, MOD_GATE, d, tpb),
    ]
    out_shape = [jax.ShapeDtypeStruct((m, d), F32)]
    out_specs = [pl.BlockSpec((tm, d), lambda i: (i, 0))]
    args = [a, b, w_out, x2d, mod]
    if next_w_in is not None:
        rows_w, d_in = next_w_in.shape[1:]
        slab = rows_w // (m // tm)
        assert slab * (m // tm) == rows_w and slab % 16 == 0
        in_specs.append(pl.BlockSpec((None, slab, d_in), lambda i: (layer + 1, i, 0)))
        out_shape.append(jax.ShapeDtypeStruct((rows_w, d_in), BF16))
        out_specs.append(pl.BlockSpec((slab, d_in), lambda i: (i, 0)))
        args.append(next_w_in)
    outs = pl.pallas_call(
        functools.partial(_outproj_kernel, tm=tm, next_layout=next_layout if next_w_in is not None else None),
        out_shape=out_shape,
        grid=(m // tm,),
        in_specs=in_specs,
        out_specs=out_specs,
        scratch_shapes=[pltpu.VMEM((2, tm, COL_TILE), F32)],
        compiler_params=_params("arbitrary"),
        name="output_projection",
    )(*args)
    return outs if next_w_in is not None else outs[0]


def _pair_dims(t):
    lead = t.shape[:-1]
    quarter = HEAD_DIM // 4
    t = t.reshape(*lead, -1, 2, 2, quarter)
    return jnp.swapaxes(t, -3, -2).reshape(*lead, -1)


def _rope_tables(n_tokens):
    rows = n_tokens // GRID_W
    row_id = jnp.broadcast_to(jnp.arange(rows)[:, None], (rows, GRID_W)).reshape(-1)
    col_id = jnp.broadcast_to(jnp.arange(GRID_W)[None, :], (rows, GRID_W)).reshape(-1)
    axis_dim = HEAD_DIM // 2
    inv_freq = ROPE_THETA ** (-jnp.arange(0, axis_dim, 2, dtype=F32) / axis_dim)
    ang_r = row_id.astype(F32)[:, None] * inv_freq[None, :]
    ang_c = col_id.astype(F32)[:, None] * inv_freq[None, :]
    ang = _pair_dims(jnp.concatenate([ang_r, ang_r, ang_c, ang_c], axis=-1))
    sign = jnp.where(jnp.arange(HEAD_DIM) < HEAD_DIM // 2, -1.0, 1.0)
    return jnp.cos(ang), jnp.sin(ang) * sign


def _pair_lanes(t):
    quarter = HEAD_DIM // 4
    lane = lax.broadcasted_iota(jnp.int32, t.shape, 1)
    from_right = pltpu.roll(t, HEAD_DIM - quarter, 1)
    from_left = pltpu.roll(t, quarter, 1)
    second = (lane >= quarter) & (lane < 2 * quarter)
    third = (lane >= 2 * quarter) & (lane < 3 * quarter)
    return jnp.where(second, from_right, jnp.where(third, from_left, t))


def _w_in_layout_kernel(u_ref, v_ref, za_ref, zb_ref, qkv_ref, o_ref):
    n = pl.program_id(1)

    @pl.when(n < SGU_GROUPS)
    def _():
        for i, ref in enumerate((u_ref, v_ref, za_ref, zb_ref)):
            o_ref[0, :, i * LANES:(i + 1) * LANES] = ref[0].astype(BF16)

    @pl.when(n >= SGU_GROUPS)
    def _():
        is_kv_tile = n == pl.num_programs(1) - 1
        for hh in range(COL_TILE // HEAD_DIM):
            cols = slice(hh * HEAD_DIM, (hh + 1) * HEAD_DIM)
            t = qkv_ref[0, :, cols]
            paired = _pair_lanes(t)
            if hh >= N_KV_HEADS:
                paired = jnp.where(is_kv_tile, t, paired)
            o_ref[0, :, cols] = paired.astype(BF16)


def _layout_w_in(w_in, sgu_width, attn_width, depth):
    _, d, d_in = w_in.shape
    g = SGU_GROUPS
    kv_width = N_KV_HEADS * HEAD_DIM
    q0 = 3 * sgu_width
    zb0 = q0 + attn_width + 2 * kv_width
    n_tiles = d_in // COL_TILE
    assert q0 % COL_TILE == 0 and zb0 % LANES == 0 and 2 * kv_width == COL_TILE

    def branch_spec(base):
        return pl.BlockSpec((1, d, LANES), lambda l, n: (l, 0, base // LANES + jnp.minimum(n, g - 1)))

    return pl.pallas_call(
        _w_in_layout_kernel,
        out_shape=jax.ShapeDtypeStruct((depth, d, d_in), BF16),
        grid=(depth, n_tiles),
        in_specs=[branch_spec(0), branch_spec(sgu_width), branch_spec(2 * sgu_width), branch_spec(zb0),
                  pl.BlockSpec((1, d, COL_TILE), lambda l, n: (l, 0, q0 // COL_TILE + jnp.maximum(n - g, 0)))],
        out_specs=pl.BlockSpec((1, d, COL_TILE), lambda l, n: (l, 0, n)),
        compiler_params=_params("arbitrary", "arbitrary"),
        name="w_in_layout",
    )(w_in, w_in, w_in, w_in, w_in)


def kernel(x, c, ctx, c_ctx, norm_w, w_mod, b_mod, w_in, w_sgu, b_sgu, v_norm_w, q_norm_w, k_norm_w, w_out):
    batch, seq, d = x.shape
    ctx_len = ctx.shape[1]
    depth = norm_w.shape[0]
    sgu_width = SGU_GROUPS * v_norm_w.shape[-1]
    attn_width = w_out.shape[1] - sgu_width
    assert sgu_width == SGU_GROUPS * LANES and attn_width == sgu_width
    assert w_in.shape[2] == 3 * sgu_width + 2 * attn_width + 2 * N_KV_HEADS * HEAD_DIM

    rows = 8 * pl.cdiv(batch + 1, 8)
    cond = jnp.concatenate([c, c_ctx[None], jnp.zeros((rows - batch - 1, d), F32)], axis=0)
    mod = _modulation(cond, w_mod, b_mod).reshape(depth, rows, 1, 3 * d)

    tables = _rope_tables(seq)
    xl = x.reshape(batch * seq, d)
    xc = ctx.reshape(batch * ctx_len, d)
    tm_lat, tm_ctx = 512, ctx_len
    w_in_p = (_layout_w_in(w_in, sgu_width, attn_width, 1), 0)
    next_layout = _w_in_block_order(sgu_width, attn_width)

    for layer in range(depth):
        last = layer == depth - 1
        lat_rows, ctx_rows = (layer, None), (layer, batch)
        sgu_params = (w_sgu[layer], b_sgu[layer], v_norm_w[layer])
        qnw, knw = _pair_dims(q_norm_w[layer]), _pair_dims(k_norm_w[layer])

        ctx_out = _input_projection(
            xc, ctx_len, norm_w[layer], mod, ctx_rows, w_in_p, knw,
            sgu_params, None, tm=tm_ctx, kv_only=last)
        kc, vtc = ctx_out[-2:]
        if not last:
            sgu_c, zb_c, q_c = ctx_out[:3]
            attn_c = _attention(q_c, qnw, zb_c, [(kc, vtc)], seq_q=ctx_len, tq=ctx_len,
                                hp=ATTN_HEADS_PER_STEP)
            xc_next = _output_projection(sgu_c, attn_c, w_out, xc, mod, ctx_rows, ctx_len, tm=tm_ctx)

        sgu_l, zb_l, q_l, kl, vtl = _input_projection(
            xl, seq, norm_w[layer], mod, lat_rows, w_in_p, knw,
            sgu_params, tables, tm=tm_lat, kv_only=False)
        attn_l = _attention_pipelined(q_l, qnw, tables, zb_l, (kc, vtc), (kl, vtl), seq_q=seq, tq=256,
                                      hp=ATTN_HEADS_PER_STEP)
        if last:
            xl = _output_projection(sgu_l, attn_l, w_out, xl, mod, lat_rows, seq, tm=tm_lat)
        else:
            xl, w_next = _output_projection(sgu_l, attn_l, w_out, xl, mod, lat_rows, seq, tm=tm_lat,
                                            next_w_in=w_in, next_layout=next_layout)
            w_in_p = (w_next[None], 0)
            xc = xc_next
    return xl.reshape(batch, seq, d)
```

```python
import functools
import math

import jax
import jax.numpy as jnp
from jax import lax
from jax.experimental import pallas as pl
from jax.experimental.pallas import tpu as pltpu

F32 = jnp.float32
BF16 = jnp.bfloat16

GRID_W = 64
CHUNK = 128
HEAD_DIM = 128
N_KV_HEADS = 2
SGU_GROUPS = 8
ROPE_THETA = 10000.0
EPS = 1e-6

LANES = 128
SUBLANES = 8
COL_TILE = 4 * LANES
K_CHUNK = 512
NORM_ROWS = 32
VMEM_LIMIT = 56 * 1024 * 1024
ATTN_HEADS_PER_STEP = 2
QK_ROWS = 256
SM_ROWS = 128
Q_SCALE = HEAD_DIM ** -0.5 * math.log2(math.e)


def _params(*sem):
    return pltpu.CompilerParams(dimension_semantics=sem, vmem_limit_bytes=VMEM_LIMIT)


def _rms(t, w):
    return t * lax.rsqrt(jnp.mean(t * t, axis=-1, keepdims=True) + EPS) * w


def _silu(t):
    return t * jax.nn.sigmoid(t)


def _rope(t, cos, sin_signed):
    return t * cos + pltpu.roll(t, HEAD_DIM // 2, 1) * sin_signed


def _fold_rows(x, op):
    parts = [x[i:i + SUBLANES] for i in range(0, x.shape[0], SUBLANES)]
    while len(parts) > 1:
        parts = [op(a, b) for a, b in zip(parts[0::2], parts[1::2])] + ([parts[-1]] if len(parts) % 2 else [])
    return parts[0]


def _mod_kernel(cond_ref, w_ref, b_ref, o_ref):
    a = _silu(cond_ref[...]).astype(BF16)
    o_ref[0] = jnp.dot(a, w_ref[0].astype(BF16), preferred_element_type=F32) + b_ref[0]


def _modulation(cond, w_mod, b_mod, tn=1024):
    depth, d, n = w_mod.shape
    rows = cond.shape[0]
    return pl.pallas_call(
        _mod_kernel,
        out_shape=jax.ShapeDtypeStruct((depth, rows, n), F32),
        grid=(depth, n // tn),
        in_specs=[
            pl.BlockSpec((rows, d), lambda l, j: (0, 0)),
            pl.BlockSpec((1, d, tn), lambda l, j: (l, 0, j)),
            pl.BlockSpec((1, 1, tn), lambda l, j: (l, 0, j)),
        ],
        out_specs=pl.BlockSpec((1, rows, tn), lambda l, j: (l, 0, j)),
        compiler_params=_params("arbitrary", "arbitrary"),
        name="modulation",
    )(cond, w_mod, b_mod.reshape(depth, 1, n))


def _inproj_kernel(*refs, tm, rope, kv_only):
    it = iter(refs)
    x_ref, nw_ref, scale_ref, shift_ref, w_ref, knw_ref = (next(it) for _ in range(6))
    if not kv_only:
        wsgu_ref, bsgu_ref, vnw_ref = (next(it) for _ in range(3))
    if rope:
        cos_ref, sin_ref = (next(it) for _ in range(2))
    if not kv_only:
        sgu_out, zb_out, q_out = (next(it) for _ in range(3))
    k_out, vt_out, h_scr, acc_scr = (next(it) for _ in range(4))
    if not kv_only:
        vn_scr, s_scr = (next(it) for _ in range(2))

    d = h_scr.shape[1]
    n_tiles = w_ref.shape[1] // COL_TILE

    nw, gain, shift = nw_ref[...], 1.0 + scale_ref[0, 0], shift_ref[0, 0]
    for r0 in range(0, tm, NORM_ROWS):
        rows = slice(r0, r0 + NORM_ROWS)
        h_scr[rows, :] = (_rms(x_ref[rows, :], nw) * gain + shift).astype(BF16)

    def maybe_rope(t, rows):
        return _rope(t, cos_ref[rows, :], sin_ref[rows, :]) if rope else t

    def matmul_pieces(n, slot):
        cols = slice(n * COL_TILE, (n + 1) * COL_TILE)
        acc = []

        def piece(k0):
            part = jnp.dot(h_scr[:, k0:k0 + K_CHUNK], w_ref[k0:k0 + K_CHUNK, cols], preferred_element_type=F32)
            acc[:] = [part if not acc else acc[0] + part]
            if k0 + K_CHUNK == d:
                acc_scr[slot] = acc[0]
        return [functools.partial(piece, k0) for k0 in range(0, d, K_CHUNK)]

    def acc_block(slot, rows, i):
        return acc_scr[slot, rows, i * LANES:(i + 1) * LANES]

    def sgu_norm_piece(g, slot, r0, mix):
        rows = slice(r0, r0 + CHUNK)
        vn_scr[slot, :, rows] = _rms(jax.nn.gelu(acc_block(slot, rows, 1)), vnw_ref[g]).astype(BF16)
        if mix:
            s_scr[slot] = jnp.dot(wsgu_ref[g], vn_scr[slot], preferred_element_type=F32) + bsgu_ref[g]

    def sgu_gate_piece(g, slot, r0):
        rows = slice(r0, r0 + CHUNK)
        cols = slice(g * LANES, (g + 1) * LANES)
        gated = jax.nn.gelu(acc_block(slot, rows, 0)) * _silu(acc_block(slot, rows, 2))
        sgu_out[rows, cols] = (gated * s_scr[slot, :, rows]).astype(BF16)
        zb_out[rows, cols] = _silu(acc_block(slot, rows, 3)).astype(BF16)

    def q_piece(qt, slot, r0):
        rows = slice(r0, r0 + CHUNK)
        for hh in range(COL_TILE // HEAD_DIM):
            c0 = qt * COL_TILE + hh * HEAD_DIM
            q_out[rows, c0:c0 + HEAD_DIM] = acc_block(slot, rows, hh)

    def kv_piece(slot, r0):
        rows = slice(r0, r0 + CHUNK)
        for hh in range(N_KV_HEADS):
            k = maybe_rope(_rms(acc_block(slot, rows, hh), knw_ref[...]), rows)
            k_out[rows, hh * HEAD_DIM:(hh + 1) * HEAD_DIM] = k.astype(BF16)
            vt_out[0, hh, :, rows] = acc_block(slot, rows, N_KV_HEADS + hh).T.astype(BF16)

    def epilogue_pieces(n, slot):
        row_starts = range(0, tm, CHUNK)
        if kv_only or n == n_tiles - 1:
            return [functools.partial(kv_piece, slot, r0) for r0 in row_starts]
        if n < SGU_GROUPS:
            last_r0 = row_starts[-1]
            return ([functools.partial(sgu_norm_piece, n, slot, r0, r0 == last_r0) for r0 in row_starts]
                    + [functools.partial(sgu_gate_piece, n, slot, r0) for r0 in row_starts])
        return [functools.partial(q_piece, n - SGU_GROUPS, slot, r0) for r0 in row_starts]

    def row_matmul_pieces(n, slot):
        cols = slice(n * COL_TILE, (n + 1) * COL_TILE)

        def piece(r0):
            rows = slice(r0, r0 + CHUNK)
            acc_scr[slot, rows, :] = jnp.dot(h_scr[rows, :], w_ref[:, cols], preferred_element_type=F32)
        return [functools.partial(piece, r0) for r0 in range(0, tm, CHUNK)]

    def spread(pieces, n_slots):
        bounds = [len(pieces) * i // n_slots for i in range(n_slots + 1)]
        return [pieces[a:b] for a, b in zip(bounds, bounds[1:])]

    if kv_only:
        own = []
        for mm, ep in zip(row_matmul_pieces(0, 0), epilogue_pieces(0, 0)):
            mm()
            for piece in own:
                piece()
            own = [ep]
        for piece in own:
            piece()
        return
    order = list(range(SGU_GROUPS)) + [n_tiles - 1] + list(range(SGU_GROUPS, n_tiles - 1))
    pending = []
    for step, n in enumerate(order):
        slot = step % 2
        mms = matmul_pieces(n, slot)
        for mm, eps in zip(mms, spread(pending, len(mms))):
            mm()
            for ep in eps:
                ep()
        pending = epilogue_pieces(n, slot)
    for ep in pending:
        ep()


MOD_SHIFT, MOD_SCALE, MOD_GATE = 0, 1, 2


def _mod_spec(mod_rows, chunk, d, tpb):
    layer, row = mod_rows
    return pl.BlockSpec((1, 1, 1, d), lambda i: (layer, (i // tpb) if row is None else row, 0, chunk))


def _input_projection(x2d, seq, norm_w, mod, mod_rows, w_in_p, k_norm_w, sgu_params,
                      rope_tables, *, tm, kv_only):
    m, d = x2d.shape
    tpb = seq // tm
    w_in_p, layer = w_in_p
    n_tiles = w_in_p.shape[2] // COL_TILE
    rope = rope_tables is not None
    g = SGU_GROUPS
    resident = pl.Buffered(1)

    const2 = lambda i: (0, 0)
    const3 = lambda i: (0, 0, 0)
    if kv_only:
        w_spec = pl.BlockSpec((None, d, COL_TILE), lambda i: (layer, 0, n_tiles - 1), pipeline_mode=resident)
    else:
        w_spec = pl.BlockSpec((None, d, n_tiles * COL_TILE), lambda i: (layer, 0, 0), pipeline_mode=resident)
    in_specs = [
        pl.BlockSpec((tm, d), lambda i: (i, 0)),
        pl.BlockSpec((1, d), const2),
        _mod_spec(mod_rows, MOD_SCALE, d, tpb),
        _mod_spec(mod_rows, MOD_SHIFT, d, tpb),
        w_spec,
        pl.BlockSpec((1, HEAD_DIM), const2),
    ]
    args = [x2d, norm_w.reshape(1, d), mod, mod, w_in_p, k_norm_w.reshape(1, HEAD_DIM)]
    if not kv_only:
        w_sgu, b_sgu, v_norm_w = sgu_params
        in_specs += [
            pl.BlockSpec((g, CHUNK, CHUNK), const3),
            pl.BlockSpec((g, CHUNK, 1), const3),
            pl.BlockSpec((g, 1, LANES), const3),
        ]
        args += [w_sgu.astype(BF16), b_sgu.reshape(g, CHUNK, 1), v_norm_w.reshape(g, 1, LANES)]
    if rope:
        in_specs += [pl.BlockSpec((tm, HEAD_DIM), lambda i: (i % tpb, 0))] * 2
        args += list(rope_tables)

    out_shape, out_specs = [], []
    if not kv_only:
        for width, dtype in ((g * LANES, BF16), (g * LANES, BF16), ((n_tiles - g - 1) * COL_TILE, F32)):
            out_shape.append(jax.ShapeDtypeStruct((m, width), dtype))
            out_specs.append(pl.BlockSpec((tm, width), lambda i: (i, 0)))
    out_shape += [jax.ShapeDtypeStruct((m, N_KV_HEADS * HEAD_DIM), BF16),
                  jax.ShapeDtypeStruct((m // seq, N_KV_HEADS, HEAD_DIM, seq), BF16)]
    out_specs += [pl.BlockSpec((tm, N_KV_HEADS * HEAD_DIM), lambda i: (i, 0)),
                  pl.BlockSpec((1, N_KV_HEADS, HEAD_DIM, tm), lambda i: (i // tpb, 0, 0, i % tpb))]

    return pl.pallas_call(
        functools.partial(_inproj_kernel, tm=tm, rope=rope, kv_only=kv_only),
        out_shape=out_shape,
        grid=(m // tm,),
        in_specs=in_specs,
        out_specs=out_specs,
        scratch_shapes=[pltpu.VMEM((tm, d), BF16), pltpu.VMEM((2, tm, COL_TILE), F32)]
        + ([] if kv_only else [pltpu.VMEM((2, CHUNK, tm), BF16), pltpu.VMEM((2, CHUNK, tm), F32)]),
        compiler_params=_params("arbitrary"),
        name="input_projection_kv" if kv_only else "input_projection",
    )(*args)


def _prepare_q(q_raw, qnw, rope_tables):
    parts = []
    for c0 in range(0, q_raw.shape[1], HEAD_DIM):
        t = _rms(q_raw[:, c0:c0 + HEAD_DIM], qnw)
        if rope_tables is not None:
            t = _rope(t, *rope_tables)
        parts.append((t * Q_SCALE).astype(BF16))
    return jnp.concatenate(parts, axis=0)


def _attn_kernel(*refs, n_seg, hp):
    q_ref, qnw_ref = refs[:2]
    refs = refs[1:]
    k_refs = refs[1:1 + n_seg]
    vt_refs = refs[1 + n_seg:1 + 2 * n_seg]
    zb_ref, o_ref = refs[1 + 2 * n_seg:]
    nt = (((1,), (1,)), ((), ()))
    tq = q_ref.shape[0]
    heads = [slice(hh * HEAD_DIM, (hh + 1) * HEAD_DIM) for hh in range(hp)]
    q = _prepare_q(q_ref[...], qnw_ref[...], None)
    s = [lax.dot_general(k[...], q, nt, preferred_element_type=F32) for k in k_refs]
    mx = functools.reduce(jnp.maximum, [jnp.max(t, axis=0, keepdims=True) for t in s])
    p = [jnp.exp2(t - mx) for t in s]
    den = functools.reduce(jnp.add, [jnp.sum(t, axis=0, keepdims=True) for t in p])
    o_t = functools.reduce(jnp.add, [
        jnp.dot(vt[0, 0], t.astype(BF16), preferred_element_type=F32) for vt, t in zip(vt_refs, p)])
    o_t = o_t * (1.0 / den)
    for hh, sl in enumerate(heads):
        o = o_t[:, hh * tq:(hh + 1) * tq].T
        o_ref[:, sl] = (o * zb_ref[:, sl].astype(F32)).astype(BF16)


def _attention(q, q_norm_w, zb, segments, *, seq_q, tq, hp):
    m, width = q.shape
    n_heads = width // HEAD_DIM
    rep = n_heads // N_KV_HEADS
    nqt = seq_q // tq
    batch = m // seq_q
    assert rep % hp == 0
    q_spec = pl.BlockSpec((tq, hp * HEAD_DIM), lambda b, h, i: (b * nqt + i, h))
    k_specs, vt_specs, ks, vts = [], [], [], []
    for k, vt in segments:
        n = vt.shape[-1]
        k_specs.append(pl.BlockSpec((n, HEAD_DIM), lambda b, h, i: (b, (h * hp) // rep)))
        vt_specs.append(pl.BlockSpec((1, 1, HEAD_DIM, n), lambda b, h, i: (b, (h * hp) // rep, 0, 0)))
        ks.append(k)
        vts.append(vt)
    return pl.pallas_call(
        functools.partial(_attn_kernel, n_seg=len(segments), hp=hp),
        out_shape=jax.ShapeDtypeStruct((m, width), BF16),
        grid=(batch, n_heads // hp, nqt),
        in_specs=[q_spec, pl.BlockSpec((1, HEAD_DIM), lambda b, h, i: (0, 0))] + k_specs + vt_specs + [q_spec],
        out_specs=q_spec,
        compiler_params=_params("arbitrary", "arbitrary", "arbitrary"),
        name="attention",
    )(q, q_norm_w.reshape(1, HEAD_DIM), *ks, *vts, zb)


def _attn_pipelined_kernel(q_ref, cos_ref, sin_ref, qnw_ref, kc_ref, kl_ref, vtc_ref, vtl_ref, zb_ref, o_ref,
                           qn_even, s_even, mx_even, acc_even, den_even,
                           qn_odd, s_odd, mx_odd, acc_odd, den_odd, p_scr, *, hp):
    t = pl.program_id(0)
    nt = (((1,), (1,)), ((), ()))
    tq = q_ref.shape[0]
    n_ctx = kc_ref.shape[0]
    tiles_per_step = qn_even.shape[0]

    @pl.when(t == 0)
    def _():
        qn_odd[...] = jnp.zeros_like(qn_odd)
        s_even[...] = jnp.zeros_like(s_even)
        mx_even[...] = jnp.zeros_like(mx_even)
        acc_odd[...] = jnp.zeros_like(acc_odd)
        den_odd[...] = jnp.ones_like(den_odd)

    def tile(ti, qn_w, qn_r, s_w, mx_w, s_r, mx_r, acc_w, den_w, acc_r, den_r):
        heads = [slice((ti * hp + hh) * HEAD_DIM, (ti * hp + hh + 1) * HEAD_DIM) for hh in range(hp)]
        tile_cols = slice(ti * hp * HEAD_DIM, (ti + 1) * hp * HEAD_DIM)
        s_w, mx_w, s_r, mx_r, acc_w = s_w.at[ti], mx_w.at[ti], s_r.at[ti], mx_r.at[ti], acc_w.at[ti]
        o_t = acc_r[ti] * (1.0 / den_r[ti])
        for hh, sl in enumerate(heads):
            o = o_t[:, hh * tq:(hh + 1) * tq].T
            o_ref[:, sl] = (o * zb_ref[:, sl].astype(F32)).astype(BF16)
        qn_w[ti] = _prepare_q(q_ref[:, tile_cols], qnw_ref[...], (cos_ref[...], sin_ref[...]))
        q = qn_r[ti]
        mx = mx_r[...]
        mx8 = den8 = None
        chunks = [(0, kc_ref[...], vtc_ref[0, 0])] + [
            (n_ctx + r0, kl_ref[r0:r0 + QK_ROWS, :], vtl_ref[0, 0, :, r0:r0 + QK_ROWS])
            for r0 in range(0, kl_ref.shape[0], QK_ROWS)]

        def score_chunk(ci):
            nonlocal mx8
            r0, k, _ = chunks[ci]
            sc = lax.dot_general(k, q, nt, preferred_element_type=F32)
            s_w[r0:r0 + k.shape[0], :] = sc
            m8 = _fold_rows(sc, jnp.maximum)
            mx8 = m8 if mx8 is None else jnp.maximum(mx8, m8)

        for ci, (r0, k, vt) in enumerate(chunks):
            score_chunk(ci)
            rows = k.shape[0]
            for r1 in range(r0, r0 + rows, SM_ROWS):
                pc = jnp.exp2(s_r[r1:r1 + SM_ROWS, :] - mx)
                d8 = _fold_rows(pc, jnp.add)
                den8 = d8 if den8 is None else den8 + d8
                p_scr[r1:r1 + SM_ROWS, :] = pc.astype(BF16)
            pv = jnp.dot(vt, p_scr[r0:r0 + rows, :], preferred_element_type=F32)
            if ci == 0:
                acc_w[...] = pv
            else:
                acc_w[...] += pv
        den_w[ti] = jnp.sum(den8, axis=0, keepdims=True)
        mx_w[...] = jnp.max(mx8, axis=0, keepdims=True)

    def step(*buffers):
        for ti in range(tiles_per_step):
            pl.when(t + ti >= 0)(functools.partial(tile, ti, *buffers))

    even = (qn_even, s_even, mx_even, acc_even, den_even)
    odd = (qn_odd, s_odd, mx_odd, acc_odd, den_odd)

    def run(cur, oth):
        step(cur[0], oth[0], oth[1], oth[2], cur[1], cur[2], cur[3], cur[4], oth[3], oth[4])

    pl.when(t % 2 == 0)(lambda: run(even, odd))
    pl.when(t % 2 == 1)(lambda: run(odd, even))


def _attention_pipelined(q, q_norm_w, rope_tables, zb, ctx_seg, lat_seg, *, seq_q, tq, hp):
    m, width = q.shape
    n_heads = width // HEAD_DIM
    rep = n_heads // N_KV_HEADS
    assert rep % hp == 0
    tiles_per_step = rep // hp
    nqt = seq_q // tq
    n_steps = (m // seq_q) * N_KV_HEADS * nqt
    (kc, vtc), (kl, vtl) = ctx_seg, lat_seg
    n_ctx, n_lat = vtc.shape[-1], vtl.shape[-1]

    def decode(step):
        return step // (N_KV_HEADS * nqt), (step // nqt) % N_KV_HEADS, step % nqt

    def stage(lag):
        return lambda t: decode(jnp.clip(t - lag, 0, n_steps - 1))

    prep, score, finish, emit = stage(0), stage(1), stage(2), stage(3)

    def q_idx(b, g, qi):
        return (b * nqt + qi, g)

    def kv_idx(b, g, qi):
        return (b, g)

    q_block = (tq, rep * HEAD_DIM)
    n_cols = hp * tq
    rope_spec = pl.BlockSpec((tq, HEAD_DIM), lambda t: (prep(t)[2], 0))
    tile_scratch = [pltpu.VMEM((tiles_per_step, n_cols, HEAD_DIM), BF16),
                    pltpu.VMEM((tiles_per_step, n_ctx + n_lat, n_cols), F32),
                    pltpu.VMEM((tiles_per_step, 1, n_cols), F32),
                    pltpu.VMEM((tiles_per_step, HEAD_DIM, n_cols), F32),
                    pltpu.VMEM((tiles_per_step, 1, n_cols), F32)]
    return pl.pallas_call(
        functools.partial(_attn_pipelined_kernel, hp=hp),
        out_shape=jax.ShapeDtypeStruct((m, width), BF16),
        grid=(n_steps + 3,),
        in_specs=[
            pl.BlockSpec(q_block, lambda t: q_idx(*prep(t))),
            rope_spec,
            rope_spec,
            pl.BlockSpec((1, HEAD_DIM), lambda t: (0, 0)),
            pl.BlockSpec((n_ctx, HEAD_DIM), lambda t: kv_idx(*score(t))),
            pl.BlockSpec((n_lat, HEAD_DIM), lambda t: kv_idx(*score(t))),
            pl.BlockSpec((1, 1, HEAD_DIM, n_ctx), lambda t: kv_idx(*finish(t)) + (0, 0)),
            pl.BlockSpec((1, 1, HEAD_DIM, n_lat), lambda t: kv_idx(*finish(t)) + (0, 0)),
            pl.BlockSpec(q_block, lambda t: q_idx(*emit(t))),
        ],
        out_specs=pl.BlockSpec(q_block, lambda t: q_idx(*emit(t))),
        scratch_shapes=tile_scratch * 2 + [pltpu.VMEM((n_ctx + n_lat, n_cols), BF16)],
        compiler_params=_params("arbitrary"),
        name="attention_pipelined",
    )(q, *rope_tables, q_norm_w.reshape(1, HEAD_DIM), kc, kl, vtc, vtl, zb)


def _outproj_kernel(*refs, tm, next_layout):
    if next_layout is None:
        a_ref, b_ref, w_ref, x_ref, g_ref, o_ref, acc_scr = refs
    else:
        a_ref, b_ref, w_ref, x_ref, g_ref, wn_ref, o_ref, wn_out, acc_scr = refs
    ka = a_ref.shape[1]
    d_in, d_out = w_ref.shape

    def layout_piece(dst, src, paired):
        t = wn_ref[:, src * LANES:(src + 1) * LANES]
        wn_out[:, dst * LANES:(dst + 1) * LANES] = (_pair_lanes(t) if paired else t).astype(BF16)

    layout_pieces = [functools.partial(layout_piece, dst, src, paired)
                     for dst, (src, paired) in enumerate(next_layout or [])]

    def matmul_pieces(n, slot):
        cols = slice(n * COL_TILE, (n + 1) * COL_TILE)
        acc = []

        def piece(k0):
            src, off = (a_ref, k0) if k0 < ka else (b_ref, k0 - ka)
            w = w_ref[k0:k0 + K_CHUNK, cols].astype(BF16)
            part = jnp.dot(src[:, off:off + K_CHUNK], w, preferred_element_type=F32)
            acc[:] = [part if not acc else acc[0] + part]
            if k0 + K_CHUNK == d_in:
                acc_scr[slot] = acc[0]
        return [functools.partial(piece, k0) for k0 in range(0, d_in, K_CHUNK)]

    def residual_piece(n, slot, r0):
        rows, cols = slice(r0, r0 + CHUNK), slice(n * COL_TILE, (n + 1) * COL_TILE)
        o_ref[rows, cols] = x_ref[rows, cols] + g_ref[0, 0, :, cols] * acc_scr[slot, rows, :]

    n_col_tiles = d_out // COL_TILE
    layout_groups = [layout_pieces[len(layout_pieces) * i // n_col_tiles:len(layout_pieces) * (i + 1) // n_col_tiles]
                     for i in range(n_col_tiles)]
    pending = []
    for n in range(n_col_tiles):
        slot = n % 2
        fillers = pending + layout_groups[n]
        mms = matmul_pieces(n, slot)
        bounds = [len(fillers) * i // len(mms) for i in range(len(mms) + 1)]
        for mm, lo, hi in zip(mms, bounds, bounds[1:]):
            mm()
            for piece in fillers[lo:hi]:
                piece()
        pending = [functools.partial(residual_piece, n, slot, r0) for r0 in range(0, tm, CHUNK)]
    for ep in pending:
        ep()


def _w_in_block_order(sgu_width, attn_width):
    g = SGU_GROUPS
    q0 = 3 * sgu_width // LANES
    k0 = q0 + attn_width // LANES
    v0 = k0 + N_KV_HEADS
    zb0 = v0 + N_KV_HEADS
    order = []
    for grp in range(g):
        order += [(grp, False), (g + grp, False), (2 * g + grp, False), (zb0 + grp, False)]
    order += [(blk, True) for blk in range(q0, v0)]
    order += [(blk, False) for blk in range(v0, zb0)]
    return order


def _output_projection(a, b, w_out, x2d, mod, mod_rows, seq, *, tm, next_w_in=None, next_layout=None):
    m, d = x2d.shape
    tpb = seq // tm
    layer = mod_rows[0]
    in_specs = [
        pl.BlockSpec((tm, a.shape[1]), lambda i: (i, 0)),
        pl.BlockSpec((tm, b.shape[1]), lambda i: (i, 0)),
        pl.BlockSpec((None,) + w_out.shape[1:], lambda i: (layer, 0, 0), pipeline_mode=pl.Buffered(1)),
        pl.BlockSpec((tm, d), lambda i: (i, 0)),
        _mod_spec(mod_rows, MOD_GATE, d, tpb),
    ]
    out_shape = [jax.ShapeDtypeStruct((m, d), F32)]
    out_specs = [pl.BlockSpec((tm, d), lambda i: (i, 0))]
    args = [a, b, w_out, x2d, mod]
    if next_w_in is not None:
        rows_w, d_in = next_w_in.shape[1:]
        slab = rows_w // (m // tm)
        assert slab * (m // tm) == rows_w and slab % 16 == 0
        in_specs.append(pl.BlockSpec((None, slab, d_in), lambda i: (layer + 1, i, 0)))
        out_shape.append(jax.ShapeDtypeStruct((rows_w, d_in), BF16))
        out_specs.append(pl.BlockSpec((slab, d_in), lambda i: (i, 0)))
        args.append(next_w_in)
    outs = pl.pallas_call(
        functools.partial(_outproj_kernel, tm=tm, next_layout=next_layout if next_w_in is not None else None),
        out_shape=out_shape,
        grid=(m // tm,),
        in_specs=in_specs,
        out_specs=out_specs,
        scratch_shapes=[pltpu.VMEM((2, tm, COL_TILE), F32)],
        compiler_params=_params("arbitrary"),
        name="output_projection",
    )(*args)
    return outs if next_w_in is not None else outs[0]


def _pair_dims(t):
    lead = t.shape[:-1]
    quarter = HEAD_DIM // 4
    t = t.reshape(*lead, -1, 2, 2, quarter)
    return jnp.swapaxes(t, -3, -2).reshape(*lead, -1)


def _rope_tables(n_tokens):
    rows = n_tokens // GRID_W
    row_id = jnp.broadcast_to(jnp.arange(rows)[:, None], (rows, GRID_W)).reshape(-1)
    col_id = jnp.broadcast_to(jnp.arange(GRID_W)[None, :], (rows, GRID_W)).reshape(-1)
    axis_dim = HEAD_DIM // 2
    inv_freq = ROPE_THETA ** (-jnp.arange(0, axis_dim, 2, dtype=F32) / axis_dim)
    ang_r = row_id.astype(F32)[:, None] * inv_freq[None, :]
    ang_c = col_id.astype(F32)[:, None] * inv_freq[None, :]
    ang = _pair_dims(jnp.concatenate([ang_r, ang_r, ang_c, ang_c], axis=-1))
    sign = jnp.where(jnp.arange(HEAD_DIM) < HEAD_DIM // 2, -1.0, 1.0)
    return jnp.cos(ang), jnp.sin(ang) * sign


def _pair_lanes(t):
    quarter = HEAD_DIM // 4
    lane = lax.broadcasted_iota(jnp.int32, t.shape, 1)
    from_right = pltpu.roll(t, HEAD_DIM - quarter, 1)
    from_left = pltpu.roll(t, quarter, 1)
    second = (lane >= quarter) & (lane < 2 * quarter)
    third = (lane >= 2 * quarter) & (lane < 3 * quarter)
    return jnp.where(second, from_right, jnp.where(third, from_left, t))


def _w_in_layout_kernel(u_ref, v_ref, za_ref, zb_ref, qkv_ref, o_ref):
    n = pl.program_id(1)

    @pl.when(n < SGU_GROUPS)
    def _():
        for i, ref in enumerate((u_ref, v_ref, za_ref, zb_ref)):
            o_ref[0, :, i * LANES:(i + 1) * LANES] = ref[0].astype(BF16)

    @pl.when(n >= SGU_GROUPS)
    def _():
        is_kv_tile = n == pl.num_programs(1) - 1
        for hh in range(COL_TILE // HEAD_DIM):
            cols = slice(hh * HEAD_DIM, (hh + 1) * HEAD_DIM)
            t = qkv_ref[0, :, cols]
            paired = _pair_lanes(t)
            if hh >= N_KV_HEADS:
                paired = jnp.where(is_kv_tile, t, paired)
            o_ref[0, :, cols] = paired.astype(BF16)


def _layout_w_in(w_in, sgu_width, attn_width, depth):
    _, d, d_in = w_in.shape
    g = SGU_GROUPS
    kv_width = N_KV_HEADS * HEAD_DIM
    q0 = 3 * sgu_width
    zb0 = q0 + attn_width + 2 * kv_width
    n_tiles = d_in // COL_TILE
    assert q0 % COL_TILE == 0 and zb0 % LANES == 0 and 2 * kv_width == COL_TILE

    def branch_spec(base):
        return pl.BlockSpec((1, d, LANES), lambda l, n: (l, 0, base // LANES + jnp.minimum(n, g - 1)))

    return pl.pallas_call(
        _w_in_layout_kernel,
        out_shape=jax.ShapeDtypeStruct((depth, d, d_in), BF16),
        grid=(depth, n_tiles),
        in_specs=[branch_spec(0), branch_spec(sgu_width), branch_spec(2 * sgu_width), branch_spec(zb0),
                  pl.BlockSpec((1, d, COL_TILE), lambda l, n: (l, 0, q0 // COL_TILE + jnp.maximum(n - g, 0)))],
        out_specs=pl.BlockSpec((1, d, COL_TILE), lambda l, n: (l, 0, n)),
        compiler_params=_params("arbitrary", "arbitrary"),
        name="w_in_layout",
    )(w_in, w_in, w_in, w_in, w_in)


def kernel(x, c, ctx, c_ctx, norm_w, w_mod, b_mod, w_in, w_sgu, b_sgu, v_norm_w, q_norm_w, k_norm_w, w_out):
    batch, seq, d = x.shape
    ctx_len = ctx.shape[1]
    depth = norm_w.shape[0]
    sgu_width = SGU_GROUPS * v_norm_w.shape[-1]
    attn_width = w_out.shape[1] - sgu_width
    assert sgu_width == SGU_GROUPS * LANES and attn_width == sgu_width
    assert w_in.shape[2] == 3 * sgu_width + 2 * attn_width + 2 * N_KV_HEADS * HEAD_DIM

    rows = 8 * pl.cdiv(batch + 1, 8)
    cond = jnp.concatenate([c, c_ctx[None], jnp.zeros((rows - batch - 1, d), F32)], axis=0)
    mod = _modulation(cond, w_mod, b_mod).reshape(depth, rows, 1, 3 * d)

    tables = _rope_tables(seq)
    xl = x.reshape(batch * seq, d)
    xc = ctx.reshape(batch * ctx_len, d)
    tm_lat, tm_ctx = 512, ctx_len
    w_in_p = (_layout_w_in(w_in, sgu_width, attn_width, 1), 0)
    next_layout = _w_in_block_order(sgu_width, attn_width)

    for layer in range(depth):
        last = layer == depth - 1
        lat_rows, ctx_rows = (layer, None), (layer, batch)
        sgu_params = (w_sgu[layer], b_sgu[layer], v_norm_w[layer])
        qnw, knw = _pair_dims(q_norm_w[layer]), _pair_dims(k_norm_w[layer])

        ctx_out = _input_projection(
            xc, ctx_len, norm_w[layer], mod, ctx_rows, w_in_p, knw,
            sgu_params, None, tm=tm_ctx, kv_only=last)
        kc, vtc = ctx_out[-2:]
        if not last:
            sgu_c, zb_c, q_c = ctx_out[:3]
            attn_c = _attention(q_c, qnw, zb_c, [(kc, vtc)], seq_q=ctx_len, tq=ctx_len,
                                hp=ATTN_HEADS_PER_STEP)
            xc_next = _output_projection(sgu_c, attn_c, w_out, xc, mod, ctx_rows, ctx_len, tm=tm_ctx)

        sgu_l, zb_l, q_l, kl, vtl = _input_projection(
            xl, seq, norm_w[layer], mod, lat_rows, w_in_p, knw,
            sgu_params, tables, tm=tm_lat, kv_only=False)
        attn_l = _attention_pipelined(q_l, qnw, tables, zb_l, (kc, vtc), (kl, vtl), seq_q=seq, tq=256,
                                      hp=ATTN_HEADS_PER_STEP)
        if last:
            xl = _output_projection(sgu_l, attn_l, w_out, xl, mod, lat_rows, seq, tm=tm_lat)
        else:
            xl, w_next = _output_projection(sgu_l, attn_l, w_out, xl, mod, lat_rows, seq, tm=tm_lat,
                                            next_w_in=w_in, next_layout=next_layout)
            w_in_p = (w_next[None], 0)
            xc = xc_next
    return xl.reshape(batch, seq, d)
```
